```python
import math
import jax, jax.numpy as jnp
from jax import lax
import numpy as np

D_MODEL = 1024
BATCH = 2
SEQ = 8192
DEPTH = 1
DEC_BATCH = 128
DEC_SEQ = 1
PAST_LEN = 2048
PAGE_SIZE = 128

HEAD_DIM = 64
MOBA_HEADS = 8
MOBA_WIDTH = MOBA_HEADS * HEAD_DIM
MOBA_BLOCK = 256
MOBA_TOPK = 3
DIFF_HEADS = 4
DIFF_QK_DIM = 64
DIFF_V_DIM = 2 * DIFF_QK_DIM
DIFF_WIDTH = DIFF_HEADS * DIFF_V_DIM
MIX_WIDTH = MOBA_WIDTH + DIFF_WIDTH
C_QM = 0
C_KM = C_QM + MOBA_WIDTH
C_VM = C_KM + MOBA_WIDTH
C_QD = C_VM + MOBA_WIDTH
C_KD = C_QD + DIFF_HEADS * 2 * DIFF_QK_DIM
C_VD = C_KD + DIFF_HEADS * 2 * DIFF_QK_DIM
IN_WIDTH = C_VD + DIFF_WIDTH
ROT_DIM = HEAD_DIM // 4
ROPE_THETA = 500000.0
Q_BLOCK = 128
N_GROUPS = 4
EXPERTS_PER_GROUP = 4
N_EXPERTS = N_GROUPS * EXPERTS_PER_GROUP
EXPERT_TOPK = 2
D_EXPERT = 512
ALPHA = (2 * DEPTH) ** 0.25
BETA = (8 * DEPTH) ** -0.25
LN_EPS = 1e-5
RMS_EPS = 1e-5

kernel_name = 'hymba_moba_diffattn_hmoe_step'


def lambda_init(layer):
    return 0.8 - 0.6 * math.exp(-0.3 * layer)


def rope(x, pos):
    half = ROT_DIM // 2
    inv_freq = ROPE_THETA ** (-jnp.arange(half, dtype=jnp.float32) / half)
    ang = pos.astype(jnp.float32)[:, None] * inv_freq[None, :]
    shape = (1, pos.shape[0]) + (1,) * (x.ndim - 3) + (half,)
    cos = jnp.cos(ang).reshape(shape).astype(x.dtype)
    sin = jnp.sin(ang).reshape(shape).astype(x.dtype)
    x1 = x[..., :half]
    x2 = x[..., half:ROT_DIM]
    return jnp.concatenate([x1 * cos - x2 * sin, x2 * cos + x1 * sin, x[..., ROT_DIM:]], axis=-1)


def project(x, pos, w_in):
    B, S, _ = x.shape
    h = jnp.einsum('bsd,de->bse', x, w_in)
    qm = h[..., C_QM:C_KM].reshape(B, S, MOBA_HEADS, HEAD_DIM)
    km = h[..., C_KM:C_VM].reshape(B, S, MOBA_HEADS, HEAD_DIM)
    vm = h[..., C_VM:C_QD].reshape(B, S, MOBA_HEADS, HEAD_DIM)
    qd = h[..., C_QD:C_KD].reshape(B, S, DIFF_HEADS, 2, DIFF_QK_DIM)
    kd = h[..., C_KD:C_VD].reshape(B, S, DIFF_HEADS, 2, DIFF_QK_DIM)
    vd = h[..., C_VD:IN_WIDTH].reshape(B, S, DIFF_HEADS, DIFF_V_DIM)
    return rope(qm, pos), rope(km, pos), vm, rope(qd, pos), rope(kd, pos), vd


def block_pad(t):
    pad = (-t.shape[1]) % MOBA_BLOCK
    if pad:
        t = jnp.pad(t, [(0, 0), (0, pad)] + [(0, 0)] * (t.ndim - 2))
    return t


def block_means(kp):
    B, Lp, H, D = kp.shape
    nb = Lp // MOBA_BLOCK
    m = kp.reshape(B, nb, MOBA_BLOCK, H, D).astype(jnp.float32).mean(axis=2).astype(kp.dtype)
    if nb < MOBA_TOPK:
        m = jnp.pad(m, ((0, 0), (0, MOBA_TOPK - nb), (0, 0), (0, 0)))
    return m


def moba_attend(q, qpos, own_blk, kp, vp, kmean):
    B, C, H, D = q.shape
    nb = kp.shape[1] // MOBA_BLOCK
    scale = D ** -0.5
    gate = jnp.einsum('bchd,bnhd->bchn', q, kmean).astype(jnp.float32)
    is_past = jnp.arange(kmean.shape[1]) < own_blk
    gate = jnp.where(is_past, gate, -jnp.inf)
    _, idx = lax.top_k(gate, MOBA_TOPK)
    valid = idx < own_blk
    idx = jnp.minimum(idx, nb - 1)
    kb = kp.reshape(B, nb, MOBA_BLOCK, H, D)
    vb = vp.reshape(B, nb, MOBA_BLOCK, H, D)
    bi = jnp.arange(B)[:, None, None, None]
    hi = jnp.arange(H)[None, None, :, None]
    kg = kb[bi, idx, :, hi]
    vg = vb[bi, idx, :, hi]
    n_sel = MOBA_TOPK * MOBA_BLOCK
    s_sel = jnp.einsum('bchd,bchjtd->bchjt', q, kg).astype(jnp.float32) * scale
    s_sel = jnp.where(valid[..., None], s_sel, -jnp.inf).reshape(B, C, H, n_sel)
    start = own_blk * MOBA_BLOCK
    k_own = lax.dynamic_slice_in_dim(kp, start, MOBA_BLOCK, axis=1)
    v_own = lax.dynamic_slice_in_dim(vp, start, MOBA_BLOCK, axis=1)
    s_own = jnp.einsum('bchd,bthd->bcht', q, k_own).astype(jnp.float32) * scale
    causal = (start + jnp.arange(MOBA_BLOCK))[None, :] <= qpos[:, None]
    s_own = jnp.where(causal[None, :, None, :], s_own, -jnp.inf)
    p = jax.nn.softmax(jnp.concatenate([s_sel, s_own], axis=-1), axis=-1).astype(vp.dtype)
    p_sel = p[..., :n_sel].reshape(B, C, H, MOBA_TOPK, MOBA_BLOCK)
    return (jnp.einsum('bchjt,bchjtd->bchd', p_sel, vg)
            + jnp.einsum('bcht,bthd->bchd', p[..., n_sel:], v_own))


def moba_prompt(q, k, v):
    B, S, H, D = q.shape
    kp, vp = block_pad(k), block_pad(v)
    kmean = block_means(kp)
    nc = S // Q_BLOCK
    qc = q.reshape(B, nc, Q_BLOCK, H, D).transpose(1, 0, 2, 3, 4)

    def step(args):
        qi, i = args
        qpos = i * Q_BLOCK + jnp.arange(Q_BLOCK)
        own = (i * Q_BLOCK) // MOBA_BLOCK
        return moba_attend(qi, qpos, own, kp, vp, kmean)

    o = lax.map(step, (qc, jnp.arange(nc, dtype=jnp.int32)))
    return o.transpose(1, 0, 2, 3, 4).reshape(B, S, H, D)


def diff_attend(q, qpos, k, v, kpos, lam):
    scale = DIFF_QK_DIM ** -0.5
    s = jnp.einsum('bchrd,blhrd->bhrcl', q, k).astype(jnp.float32) * scale
    mask = kpos[None, :] <= qpos[:, None]
    p = jax.nn.softmax(jnp.where(mask, s, -jnp.inf), axis=-1)
    a = (p[:, :, 0] - lam * p[:, :, 1]).astype(v.dtype)
    return jnp.einsum('bhcl,blhe->bche', a, v)


def diff_prompt(q, k, v, lam):
    B, S = q.shape[:2]
    nc = S // Q_BLOCK
    qc = q.reshape((B, nc, Q_BLOCK) + q.shape[2:]).transpose(1, 0, 2, 3, 4, 5)
    kpos = jnp.arange(S)

    def step(args):
        qi, i = args
        return diff_attend(qi, i * Q_BLOCK + jnp.arange(Q_BLOCK), k, v, kpos, lam)

    o = lax.map(step, (qc, jnp.arange(nc, dtype=jnp.int32)))
    return o.transpose(1, 0, 2, 3, 4).reshape(B, S, DIFF_HEADS, DIFF_V_DIM)


def heads_rmsnorm(o, g):
    of = o.astype(jnp.float32)
    of = of * lax.rsqrt(jnp.mean(of * of, axis=-1, keepdims=True) + RMS_EPS)
    return (of * g.astype(jnp.float32)).astype(o.dtype)


def layer_norm(x, g, b):
    xf = x.astype(jnp.float32)
    mu = jnp.mean(xf, axis=-1, keepdims=True)
    var = jnp.mean(jnp.square(xf - mu), axis=-1, keepdims=True)
    y = (xf - mu) * lax.rsqrt(var + LN_EPS) * g.astype(jnp.float32) + b.astype(jnp.float32)
    return y.astype(x.dtype)


def hier_moe(x, w_group, b_group, w_expert, b_expert, w_gate, w_up, w_down):
    shp = x.shape
    xf = x.reshape(-1, D_MODEL)
    n = xf.shape[0]
    gp = jax.nn.softmax((xf @ w_group + b_group).astype(jnp.float32), axis=-1)
    g_idx = jnp.argmax(gp, axis=-1)
    g_w = jnp.max(gp, axis=-1)
    el = (xf @ w_expert + b_expert).astype(jnp.float32).reshape(n, N_GROUPS, EXPERTS_PER_GROUP)
    el = jnp.take_along_axis(el, g_idx[:, None, None], axis=1)[:, 0]
    tv, ti = lax.top_k(jax.nn.softmax(el, axis=-1), EXPERT_TOPK)
    wts = g_w[:, None] * tv / jnp.sum(tv, axis=-1, keepdims=True)
    eid = g_idx[:, None] * EXPERTS_PER_GROUP + ti
    gates = jnp.sum(jax.nn.one_hot(eid, N_EXPERTS, dtype=jnp.float32) * wts[..., None], axis=1).astype(x.dtype)
    h = jax.nn.silu(jnp.einsum('nd,edf->nef', xf, w_gate)) * jnp.einsum('nd,edf->nef', xf, w_up)
    y = jnp.einsum('nef,efd->nd', h * gates[:, :, None], w_down)
    return y.reshape(shp)


def finish(x, om, od, w_out, g_moba, ln1_g, ln1_b, w_group, b_group, w_expert, b_expert,
           w_gate, w_up, w_down, ln2_g, ln2_b):
    B, S, _ = x.shape
    mixed = jnp.concatenate([heads_rmsnorm(om, g_moba).reshape(B, S, MOBA_WIDTH),
                             od.reshape(B, S, DIFF_WIDTH)], axis=-1)
    x1 = layer_norm(ALPHA * x + jnp.einsum('bse,ed->bsd', mixed, w_out), ln1_g, ln1_b)
    ff = hier_moe(x1, w_group, b_group, w_expert, b_expert, w_gate, w_up, w_down)
    return layer_norm(ALPHA * x1 + ff, ln2_g, ln2_b)


def gather_pages(cache, page_table):
    g = cache[page_table]
    return g.reshape((g.shape[0], g.shape[1] * g.shape[2]) + g.shape[3:])


def setup_inputs(seed: int = 0) -> dict:
    key = jax.random.key(seed)
    ks = jax.random.split(key, 32)
    f32 = jnp.float32
    n_pages = PAST_LEN // PAGE_SIZE
    n_used = DEC_BATCH * n_pages
    n_phys = n_used + (n_used + 3) // 4

    def nrm(k, shape, s):
        return jax.random.normal(k, shape, f32) * s

    col_scale = jnp.concatenate([jnp.ones((2 * MOBA_WIDTH,), f32), jnp.full((MOBA_WIDTH,), BETA, f32),
                                 jnp.ones((C_VD - C_QD,), f32), jnp.full((DIFF_WIDTH,), BETA, f32)])
    perm = jax.random.permutation(ks[6], n_phys)
    page_table = perm[:n_used].reshape(DEC_BATCH, n_pages).astype(jnp.int32)
    return {
        'x_prompt': nrm(ks[0], (BATCH, SEQ, D_MODEL), 1.0),
        'x_sample': nrm(ks[1], (DEC_BATCH, DEC_SEQ, D_MODEL), 1.0),
        'cache_moba_k': nrm(ks[2], (DEPTH, n_phys, PAGE_SIZE, MOBA_HEADS, HEAD_DIM), 1.0),
        'cache_moba_v': nrm(ks[3], (DEPTH, n_phys, PAGE_SIZE, MOBA_HEADS, HEAD_DIM), BETA),
        'cache_diff_k': nrm(ks[4], (DEPTH, n_phys, PAGE_SIZE, DIFF_HEADS, 2, DIFF_QK_DIM), 1.0),
        'cache_diff_v': nrm(ks[5], (DEPTH, n_phys, PAGE_SIZE, DIFF_HEADS, DIFF_V_DIM), BETA),
        'page_table': page_table,
        'w_in': nrm(ks[7], (DEPTH, D_MODEL, IN_WIDTH), D_MODEL ** -0.5) * col_scale,
        'w_out': nrm(ks[8], (DEPTH, MIX_WIDTH, D_MODEL), BETA * MIX_WIDTH ** -0.5),
        'g_moba': 1.0 + nrm(ks[9], (DEPTH, MOBA_HEADS, HEAD_DIM), 0.02),
        'g_diff': 1.0 + nrm(ks[10], (DEPTH, DIFF_HEADS, DIFF_V_DIM), 0.02),
        'lam_q1': nrm(ks[11], (DEPTH, DIFF_QK_DIM), 0.1),
        'lam_k1': nrm(ks[12], (DEPTH, DIFF_QK_DIM), 0.1),
        'lam_q2': nrm(ks[13], (DEPTH, DIFF_QK_DIM), 0.1),
        'lam_k2': nrm(ks[14], (DEPTH, DIFF_QK_DIM), 0.1),
        'ln1_g': 1.0 + nrm(ks[15], (DEPTH, D_MODEL), 0.02),
        'ln1_b': nrm(ks[16], (DEPTH, D_MODEL), 0.02),
        'w_group': nrm(ks[17], (DEPTH, D_MODEL, N_GROUPS), D_MODEL ** -0.5),
        'b_group': nrm(ks[18], (DEPTH, N_GROUPS), 0.01),
        'w_expert': nrm(ks[19], (DEPTH, D_MODEL, N_EXPERTS), D_MODEL ** -0.5),
        'b_expert': nrm(ks[20], (DEPTH, N_EXPERTS), 0.01),
        'w_gate': nrm(ks[21], (DEPTH, N_EXPERTS, D_MODEL, D_EXPERT), D_MODEL ** -0.5),
        'w_up': nrm(ks[22], (DEPTH, N_EXPERTS, D_MODEL, D_EXPERT), BETA * D_MODEL ** -0.5),
        'w_down': nrm(ks[23], (DEPTH, N_EXPERTS, D_EXPERT, D_MODEL), BETA * D_EXPERT ** -0.5),
        'ln2_g': 1.0 + nrm(ks[24], (DEPTH, D_MODEL), 0.02),
        'ln2_b': nrm(ks[25], (DEPTH, D_MODEL), 0.02),
    }


def reference(x_prompt, x_sample, cache_moba_k, cache_moba_v, cache_diff_k, cache_diff_v, page_table,
              w_in, w_out, g_moba, g_diff, lam_q1, lam_k1, lam_q2, lam_k2, ln1_g, ln1_b,
              w_group, b_group, w_expert, b_expert, w_gate, w_up, w_down, ln2_g, ln2_b):
    pos_p = jnp.arange(SEQ)
    pos_s = PAST_LEN + jnp.arange(DEC_SEQ)
    kpos_s = jnp.arange(PAST_LEN + DEC_SEQ)
    own_s = PAST_LEN // MOBA_BLOCK
    xp, xs = x_prompt, x_sample
    mk_p, mv_p, dk_p, dv_p = [], [], [], []
    mk_s, mv_s, dk_s, dv_s = [], [], [], []
    for l in range(DEPTH):
        li = lambda_init(l)
        lam = (jnp.exp(jnp.sum(lam_q1[l].astype(jnp.float32) * lam_k1[l].astype(jnp.float32)))
               - jnp.exp(jnp.sum(lam_q2[l].astype(jnp.float32) * lam_k2[l].astype(jnp.float32))) + li)
        tail = (w_out[l], g_moba[l], ln1_g[l], ln1_b[l], w_group[l], b_group[l], w_expert[l], b_expert[l],
                w_gate[l], w_up[l], w_down[l], ln2_g[l], ln2_b[l])
        qm, km, vm, qd, kd, vd = project(xp, pos_p, w_in[l])
        om = moba_prompt(qm, km, vm)
        od = heads_rmsnorm(diff_prompt(qd, kd, vd, lam), g_diff[l]) * (1.0 - li)
        xp = finish(xp, om, od, *tail)
        mk_p.append(km); mv_p.append(vm); dk_p.append(kd); dv_p.append(vd)
        qm2, km2, vm2, qd2, kd2, vd2 = project(xs, pos_s, w_in[l])
        kmf = block_pad(jnp.concatenate([gather_pages(cache_moba_k[l], page_table), km2], axis=1))
        vmf = block_pad(jnp.concatenate([gather_pages(cache_moba_v[l], page_table), vm2], axis=1))
        om2 = moba_attend(qm2, pos_s, own_s, kmf, vmf, block_means(kmf))
        kdf = jnp.concatenate([gather_pages(cache_diff_k[l], page_table), kd2], axis=1)
        vdf = jnp.concatenate([gather_pages(cache_diff_v[l], page_table), vd2], axis=1)
        od2 = heads_rmsnorm(diff_attend(qd2, pos_s, kdf, vdf, kpos_s, lam), g_diff[l]) * (1.0 - li)
        xs = finish(xs, om2, od2, *tail)
        mk_s.append(km2); mv_s.append(vm2); dk_s.append(kd2); dv_s.append(vd2)
    moba_k_prompt = jnp.stack(mk_p)
    moba_v_prompt = jnp.stack(mv_p)
    diff_k_prompt = jnp.stack(dk_p)
    diff_v_prompt = jnp.stack(dv_p)
    moba_k_sample = jnp.stack(mk_s)
    moba_v_sample = jnp.stack(mv_s)
    diff_k_sample = jnp.stack(dk_s)
    diff_v_sample = jnp.stack(dv_s)
    return (xp, xs, moba_k_prompt, moba_v_prompt, diff_k_prompt, diff_v_prompt,
            moba_k_sample, moba_v_sample, diff_k_sample, diff_v_sample)
```

```python
import functools
import math

import jax
import jax.numpy as jnp
from jax import lax
from jax.experimental import pallas as pl
from jax.experimental.pallas import tpu as pltpu

F32 = jnp.float32
BF16 = jnp.bfloat16

LANES = 128
HEAD_DIM = 64
MOBA_HEADS = 8
MOBA_BLOCK = 256
MOBA_TOPK = 3
DIFF_HEADS = 4
DIFF_QK_DIM = 64
DIFF_V_DIM = 128
GROUP_WIDTH = 512
ROT_DIM = HEAD_DIM // 4
ROT_HALF = ROT_DIM // 2
ROPE_THETA = 500000.0
N_GROUPS = 4
EXPERTS_PER_GROUP = 4
N_EXPERTS = N_GROUPS * EXPERTS_PER_GROUP
EXPERT_TOPK = 2
LN_EPS = 1e-5
RMS_EPS = 1e-5
QK_SCALE = HEAD_DIM ** -0.5
MASK_BIAS = 2.0 ** 30
ONES_ROWS = 16
VMEM_LIMIT = 48 * 1024 * 1024


def _lambda_init(layer):
    return 0.8 - 0.6 * math.exp(-0.3 * layer)


def _params(sem, vmem=VMEM_LIMIT):
    return pltpu.CompilerParams(dimension_semantics=sem, vmem_limit_bytes=vmem)


def _nt_dot(a, b):
    return lax.dot_general(a, b, (((1,), (1,)), ((), ())), preferred_element_type=F32)


def _layer_norm(z, g, b):
    mu = jnp.mean(z, axis=-1, keepdims=True)
    zc = z - mu
    var = jnp.mean(zc * zc, axis=-1, keepdims=True)
    return zc * lax.rsqrt(var + LN_EPS) * g + b


def _rope_tables(pos):
    n = pos.shape[0]
    inv_freq = ROPE_THETA ** (-jnp.arange(ROT_HALF, dtype=F32) / ROT_HALF)
    ang = pos.astype(F32)[:, None] * inv_freq[None, :]
    cos, sin = jnp.cos(ang), jnp.sin(ang)
    rest = HEAD_DIM - ROT_DIM
    c64 = jnp.concatenate([cos, cos, jnp.ones((n, rest), F32)], axis=1)
    sa64 = jnp.concatenate([-sin, jnp.zeros((n, rest + ROT_HALF), F32)], axis=1)
    sb64 = jnp.concatenate([jnp.zeros((n, ROT_HALF), F32), sin, jnp.zeros((n, rest), F32)], axis=1)
    reps = LANES // HEAD_DIM
    return jnp.tile(c64, (1, reps)), jnp.tile(sa64, (1, reps)), jnp.tile(sb64, (1, reps))


def _rope(h, cos, sa, sb):
    parts = []
    for a in range(h.shape[1] // LANES):
        ha = h[:, a * LANES:(a + 1) * LANES]
        parts.append(ha * cos + pltpu.roll(ha, LANES - ROT_HALF, 1) * sa + pltpu.roll(ha, ROT_HALF, 1) * sb)
    return jnp.concatenate(parts, axis=1)


def _proj_kernel(x_ref, w_ref, wvt_ref, cos_ref, sa_ref, sb_ref, *out_refs, prompt):
    if prompt:
        (qm_ref, qd_ref, km_ref, vm_ref, kd_ref, vd_ref,
         kmb_ref, kdb_ref, vmt_ref, vdt_ref, kmean_ref) = out_refs
    else:
        qm_ref, qd_ref, km_ref, vm_ref, kd_ref, vd_ref = out_refs
    xb = x_ref[0].astype(BF16)
    cos, sa, sb = cos_ref[...], sa_ref[...], sb_ref[...]
    w = GROUP_WIDTH

    def col(c):
        return jnp.dot(xb, w_ref[:, c * w:(c + 1) * w], preferred_element_type=F32)

    qm_ref[0] = (_rope(col(0), cos, sa, sb) * QK_SCALE).astype(BF16)
    km = _rope(col(1), cos, sa, sb)
    km_ref[0] = km
    vm_ref[0] = col(2)
    qd_ref[0] = (_rope(col(3), cos, sa, sb) * QK_SCALE).astype(BF16)
    kd = _rope(col(4), cos, sa, sb)
    kd_ref[0] = kd
    vd_ref[0] = col(5)
    if prompt:
        kmb_ref[0] = km.astype(BF16)
        kdb_ref[0] = kd.astype(BF16)
        kmean_ref[0, 0] = jnp.mean(km, axis=0, keepdims=True)
        vmt_ref[0, 0] = _nt_dot(wvt_ref[0:w, :], xb).astype(BF16)
        vdt_ref[0, 0] = _nt_dot(wvt_ref[w:2 * w, :], xb).astype(BF16)


def _proj(x, pos, w_in_b, wvt_b, *, prompt):
    B, S, D = x.shape
    tm = min(S, MOBA_BLOCK)
    nt = S // tm
    w = GROUP_WIDTH
    cos, sa, sb = _rope_tables(pos)
    row_spec = pl.BlockSpec((1, tm, w), lambda b, t: (b, t, 0))
    tab_spec = pl.BlockSpec((tm, LANES), lambda b, t: (t, 0))
    out_shape = [jax.ShapeDtypeStruct((B, S, w), BF16)] * 2 + [jax.ShapeDtypeStruct((B, S, w), F32)] * 4
    out_specs = [row_spec] * 6
    if prompt:
        vt_spec = pl.BlockSpec((1, 1, w, tm), lambda b, t: (b, t, 0, 0))
        out_shape += [jax.ShapeDtypeStruct((B, S, w), BF16)] * 2
        out_shape += [jax.ShapeDtypeStruct((B, nt, w, tm), BF16)] * 2
        out_shape += [jax.ShapeDtypeStruct((B, nt, 1, w), F32)]
        out_specs += [row_spec] * 2 + [vt_spec] * 2 + [pl.BlockSpec((1, 1, 1, w), lambda b, t: (b, t, 0, 0))]
    return pl.pallas_call(
        functools.partial(_proj_kernel, prompt=prompt),
        grid=(B, nt),
        in_specs=[pl.BlockSpec((1, tm, D), lambda b, t: (b, t, 0)),
                  pl.BlockSpec(w_in_b.shape, lambda b, t: (0, 0)),
                  pl.BlockSpec(wvt_b.shape, lambda b, t: (0, 0)),
                  tab_spec, tab_spec, tab_spec],
        out_specs=out_specs,
        out_shape=out_shape,
        compiler_params=_params(("parallel", "parallel")),
        name="proj_prompt" if prompt else "proj_sample",
    )(x, w_in_b, wvt_b, cos, sa, sb)


def _flash_step(m, acc, kp, qp, vt, causal):
    s = _nt_dot(kp, qp)
    if causal:
        krow = lax.broadcasted_iota(jnp.int32, s.shape, 0)
        qcol = lax.broadcasted_iota(jnp.int32, s.shape, 1)
        s = jnp.where(krow <= qcol, s, -jnp.inf)
    m_new = jnp.maximum(m, jnp.max(s, axis=0, keepdims=True))
    alpha = jnp.exp(m - m_new)
    p = jnp.exp(s - m_new).astype(BF16)
    lhs = jnp.concatenate([vt, jnp.ones((ONES_ROWS, vt.shape[1]), BF16)], axis=0)
    acc = alpha * acc + jnp.dot(lhs, p, preferred_element_type=F32)
    return m_new, acc


def _topk_rows(g, k):
    row = lax.broadcasted_iota(jnp.int32, g.shape, 0).astype(F32)
    none = float(g.shape[0])
    sel = jnp.zeros(g.shape, F32)
    for _ in range(k):
        mx = jnp.max(g, axis=0, keepdims=True)
        idx = jnp.min(jnp.where(g == mx, row, none), axis=0, keepdims=True)
        idx = jnp.where(mx > -jnp.inf, idx, none)
        pick = row == idx
        sel = jnp.where(pick, 1.0, sel)
        g = jnp.where(pick, -jnp.inf, g)
    return sel


def _moba_kernel(q_ref, k_ref, vt_ref, kmean_ref, g_ref, o_ref):
    i = pl.program_id(2)
    nb = k_ref.shape[1]
    tq = q_ref.shape[1]
    q = q_ref[0]
    lane = lax.broadcasted_iota(jnp.int32, (1, LANES), 1)
    kmean = kmean_ref[0].astype(BF16)
    brow = lax.broadcasted_iota(jnp.int32, (nb, tq), 0)
    outs = []
    for h in range(LANES // HEAD_DIM):
        qh = jnp.where((lane >> 6) == h, q, jnp.zeros_like(q))
        gate = _nt_dot(kmean, qh)
        sel = _topk_rows(jnp.where(brow < i, gate, -jnp.inf), MOBA_TOPK)
        bias_t = jnp.where(brow == i, 0.0, jnp.where(sel > 0.0, 0.0, -MASK_BIAS))
        if nb < LANES:
            bias_t = jnp.concatenate([bias_t, jnp.zeros((LANES - nb, tq), F32)], axis=0)
        qp = jnp.concatenate([qh, bias_t.T.astype(BF16)], axis=1)

        def keys(j):
            onehot = jnp.where(lane == j, 1.0, 0.0).astype(BF16)
            return jnp.concatenate([k_ref[0, j], jnp.broadcast_to(onehot, (MOBA_BLOCK, LANES))], axis=1)

        def vals(j, h=h):
            return vt_ref[0, j, h * HEAD_DIM:(h + 1) * HEAD_DIM, :]

        m0 = jnp.full((1, tq), -jnp.inf, F32)
        acc0 = jnp.zeros((HEAD_DIM + ONES_ROWS, tq), F32)
        carry = _flash_step(m0, acc0, keys(i), qp, vals(i), True)
        _, acc = lax.fori_loop(
            0, i, lambda j, c, qp=qp, keys=keys, vals=vals: _flash_step(c[0], c[1], keys(j), qp, vals(j), False), carry)
        o = acc[:HEAD_DIM] / acc[HEAD_DIM:HEAD_DIM + 1]
        ms = jnp.mean(o * o, axis=0, keepdims=True)
        outs.append(o * lax.rsqrt(ms + RMS_EPS))
    o_ref[0] = (jnp.concatenate(outs, axis=0).T * g_ref[...]).astype(BF16)


def _moba_prompt(qm_b, km_b, vmt_b, kmean, g_moba):
    B, S, w = qm_b.shape
    nb = S // MOBA_BLOCK
    npair = w // LANES
    k4 = km_b.reshape(B, nb, MOBA_BLOCK, w)
    return pl.pallas_call(
        _moba_kernel,
        grid=(B, npair, nb),
        in_specs=[pl.BlockSpec((1, MOBA_BLOCK, LANES), lambda b, p, i: (b, i, p)),
                  pl.BlockSpec((1, nb, MOBA_BLOCK, LANES), lambda b, p, i: (b, 0, 0, p)),
                  pl.BlockSpec((1, nb, LANES, MOBA_BLOCK), lambda b, p, i: (b, 0, p, 0)),
                  pl.BlockSpec((1, nb, LANES), lambda b, p, i: (b, 0, p)),
                  pl.BlockSpec((1, LANES), lambda b, p, i: (0, p))],
        out_specs=pl.BlockSpec((1, MOBA_BLOCK, LANES), lambda b, p, i: (b, i, p)),
        out_shape=jax.ShapeDtypeStruct((B, S, w), BF16),
        compiler_params=_params(("parallel", "parallel", "arbitrary")),
        name="moba_prompt",
    )(qm_b, k4, vmt_b, kmean.reshape(B, nb, w), g_moba.reshape(1, w))


def _lambda_value(lam_ref, lam_init):
    lp = lam_ref[...]
    a = jnp.sum(lp[0:1] * lp[1:2], axis=1, keepdims=True)
    b = jnp.sum(lp[2:3] * lp[3:4], axis=1, keepdims=True)
    return jnp.exp(a) - jnp.exp(b) + lam_init


def _diff_kernel(lam_ref, q_ref, k_ref, vt_ref, g_ref, o_ref, *, lam_init):
    i = pl.program_id(2)
    tq = q_ref.shape[1]
    lam = _lambda_value(lam_ref, lam_init)
    q = q_ref[0]
    lane = lax.broadcasted_iota(jnp.int32, (1, LANES), 1)
    outs = []
    for r in range(2):
        qr = jnp.where((lane >> 6) == r, q, jnp.zeros_like(q))
        m0 = jnp.full((1, tq), -jnp.inf, F32)
        acc0 = jnp.zeros((DIFF_V_DIM + ONES_ROWS, tq), F32)
        carry = _flash_step(m0, acc0, k_ref[0, i], qr, vt_ref[0, i], True)
        _, acc = lax.fori_loop(
            0, i, lambda j, c, qr=qr: _flash_step(c[0], c[1], k_ref[0, j], qr, vt_ref[0, j], False), carry)
        outs.append(acc[:DIFF_V_DIM] / acc[DIFF_V_DIM:DIFF_V_DIM + 1])
    o = outs[0] - lam * outs[1]
    ms = jnp.mean(o * o, axis=0, keepdims=True)
    o = o * lax.rsqrt(ms + RMS_EPS)
    o_ref[0] = ((o.T * g_ref[...]) * (1.0 - lam_init)).astype(BF16)


def _diff_prompt(lam_p, qd_b, kd_b, vdt_b, g_diff, lam_init):
    B, S, w = qd_b.shape
    nb = S // MOBA_BLOCK
    tq = MOBA_BLOCK
    k4 = kd_b.reshape(B, nb, tq, w)
    return pl.pallas_call(
        functools.partial(_diff_kernel, lam_init=lam_init),
        grid=(B, DIFF_HEADS, nb),
        in_specs=[pl.BlockSpec(lam_p.shape, lambda b, h, i: (0, 0)),
                  pl.BlockSpec((1, tq, LANES), lambda b, h, i: (b, i, h)),
                  pl.BlockSpec((1, nb, tq, LANES), lambda b, h, i: (b, 0, 0, h)),
                  pl.BlockSpec((1, nb, DIFF_V_DIM, tq), lambda b, h, i: (b, 0, h, 0)),
                  pl.BlockSpec((1, LANES), lambda b, h, i: (0, h))],
        out_specs=pl.BlockSpec((1, tq, LANES), lambda b, h, i: (b, i, h)),
        out_shape=jax.ShapeDtypeStruct((B, S, w), BF16),
        compiler_params=_params(("parallel", "parallel", "arbitrary")),
        name="diff_prompt",
    )(lam_p, qd_b, k4, vdt_b, g_diff.reshape(1, w))


SROWS = 16


def _block_partial(q_ref, k_refs, v_refs):
    r = lax.broadcasted_iota(jnp.int32, (SROWS, GROUP_WIDTH), 0)
    c = lax.broadcasted_iota(jnp.int32, (SROWS, GROUP_WIDTH), 1)
    qrow = jnp.broadcast_to(q_ref[0].astype(F32), (SROWS, GROUP_WIDTH))
    qmat = jnp.where((c >> 6) == r, qrow, 0.0).astype(BF16)
    kb = jnp.concatenate([kr[0] for kr in k_refs], axis=0).astype(BF16)
    vb = jnp.concatenate([vr[0] for vr in v_refs], axis=0).astype(BF16)
    s = _nt_dot(qmat, kb)
    m = jnp.max(s, axis=1, keepdims=True)
    p = jnp.exp(s - m)
    return (jnp.sum(s, axis=1, keepdims=True), m, jnp.sum(p, axis=1, keepdims=True),
            jnp.dot(p.astype(BF16), vb, preferred_element_type=F32))


def _topk_list(gs, k):
    none = float(len(gs))
    gs = list(gs)
    sel = [jnp.zeros(g.shape, F32) for g in gs]
    for _ in range(k):
        mx = functools.reduce(jnp.maximum, gs)
        idx = functools.reduce(jnp.minimum, [jnp.where(g == mx, float(b), none) for b, g in enumerate(gs)])
        idx = jnp.where(mx > -jnp.inf, idx, none)
        for b in range(len(gs)):
            pick = idx == float(b)
            sel[b] = jnp.where(pick, 1.0, sel[b])
            gs[b] = jnp.where(pick, -jnp.inf, gs[b])
    return sel


def _merge_partials(sel, m_sc, l_sc, r_sc, s_new, v_new, nblk):
    m = s_new
    for j in range(nblk):
        mj = m_sc[j][:, 0:1]
        m = jnp.maximum(m, mj if sel is None else jnp.where(sel[j] > 0.0, mj, -jnp.inf))
    w_new = jnp.exp(s_new - m)
    l = w_new
    racc = w_new * v_new
    for j in range(nblk):
        wj = jnp.exp(m_sc[j][:, 0:1] - m)
        if sel is not None:
            wj = jnp.where(sel[j] > 0.0, wj, 0.0)
        l = l + wj * l_sc[j][:, 0:1]
        racc = racc + wj * r_sc[j]
    return racc / l


def _decode_kernel(pt_ref, lam_ref, qm_ref, qd_ref, kmn_ref, vmn_ref, kdn_ref, vdn_ref, gm_ref, gd_ref, *rest,
                   lam_init, ppb, nblk):
    del pt_ref
    pages = rest[:4 * ppb]
    om_ref, od_ref = rest[4 * ppb:4 * ppb + 2]
    gm_sc, mm_sc, lm_sc, rm_sc, md_sc, ld_sc, rd_sc = rest[4 * ppb + 2:]
    mk, mv, dk, dv = (pages[a * ppb:(a + 1) * ppb] for a in range(4))
    j = pl.program_id(1)

    g, m, l, r = _block_partial(qm_ref, mk, mv)
    gm_sc[j] = jnp.broadcast_to(g, (SROWS, LANES))
    mm_sc[j] = jnp.broadcast_to(m, (SROWS, LANES))
    lm_sc[j] = jnp.broadcast_to(l, (SROWS, LANES))
    rm_sc[j] = r
    _, m, l, r = _block_partial(qd_ref, dk, dv)
    md_sc[j] = jnp.broadcast_to(m, (SROWS, LANES))
    ld_sc[j] = jnp.broadcast_to(l, (SROWS, LANES))
    rd_sc[j] = r

    @pl.when(j == nblk - 1)
    def _():
        r16 = lax.broadcasted_iota(jnp.int32, (SROWS, GROUP_WIDTH), 0)
        c16 = lax.broadcasted_iota(jnp.int32, (SROWS, GROUP_WIDTH), 1)

        def new_score(q_ref, k_ref):
            qf = jnp.where((c16 >> 6) == r16, jnp.broadcast_to(q_ref[0].astype(F32), (SROWS, GROUP_WIDTH)), 0.0)
            return jnp.sum(qf * k_ref[0], axis=1, keepdims=True)

        sel = _topk_list([gm_sc[b][:, 0:1] for b in range(nblk)], MOBA_TOPK)
        o = _merge_partials(sel, mm_sc, lm_sc, rm_sc, new_score(qm_ref, kmn_ref), vmn_ref[0], nblk)
        own = (c16 >> 6) == r16
        ms = jnp.sum(jnp.where(own, o * o, 0.0), axis=1, keepdims=True) * (1.0 / HEAD_DIM)
        o = jnp.where(own, o * lax.rsqrt(ms + RMS_EPS), 0.0)
        om_ref[0] = (jnp.sum(o, axis=0, keepdims=True) * gm_ref[...]).astype(BF16)

        lam = _lambda_value(lam_ref, lam_init)
        o = _merge_partials(None, md_sc, ld_sc, rd_sc, new_score(qd_ref, kdn_ref), vdn_ref[0], nblk)
        signed = jnp.where((r16 & 1) == 0, o, -lam * o)
        o = jnp.sum(jnp.where((c16 >> 7) == (r16 >> 1), signed, 0.0), axis=0, keepdims=True)
        segs = []
        for h in range(DIFF_HEADS):
            seg = o[:, h * DIFF_V_DIM:(h + 1) * DIFF_V_DIM]
            ms = jnp.mean(seg * seg, axis=1, keepdims=True)
            segs.append(seg * lax.rsqrt(ms + RMS_EPS))
        od_ref[0] = ((jnp.concatenate(segs, axis=1) * gd_ref[...]) * (1.0 - lam_init)).astype(BF16)


def _decode_attention(page_table, lam_p, qm_b, qd_b, km, vm, kd, vd, caches, g_moba, g_diff, lam_init):
    nseq, npages = page_table.shape
    page = caches[0].shape[1]
    ppb = MOBA_BLOCK // page
    nblk = npages // ppb
    w = GROUP_WIDTH
    row = pl.BlockSpec((1, 1, w), lambda b, j, pt: (b, 0, 0))
    gain = pl.BlockSpec((1, w), lambda b, j, pt: (0, 0))

    def page_spec(a):
        return pl.BlockSpec((1, page, w), lambda b, j, pt, a=a: (pt[b * npages + j * ppb + a], 0, 0))

    page_specs, page_args = [], []
    for cache in caches:
        for a in range(ppb):
            page_specs.append(page_spec(a))
            page_args.append(cache)
    stat = pltpu.VMEM((nblk, SROWS, LANES), F32)
    part = pltpu.VMEM((nblk, SROWS, w), F32)
    return pl.pallas_call(
        functools.partial(_decode_kernel, lam_init=lam_init, ppb=ppb, nblk=nblk),
        grid_spec=pltpu.PrefetchScalarGridSpec(
            num_scalar_prefetch=1,
            grid=(nseq, nblk),
            in_specs=[pl.BlockSpec(lam_p.shape, lambda b, j, pt: (0, 0)), row, row, row, row, row, row, gain, gain]
            + page_specs,
            out_specs=[row, row],
            scratch_shapes=[stat, stat, stat, part, stat, stat, part],
        ),
        out_shape=[jax.ShapeDtypeStruct((nseq, 1, w), BF16)] * 2,
        compiler_params=_params(("parallel", "arbitrary")),
        name="decode_attention",
    )(page_table.reshape(-1), lam_p, qm_b, qd_b, km, vm, kd, vd, g_moba.reshape(1, w), g_diff.reshape(1, w),
      *page_args)


def _outproj_kernel(x_ref, om_ref, od_ref, wo_ref, g_ref, b_ref, wr_ref, br_ref, x1_ref, gates_ref, *, alpha):
    w = GROUP_WIDTH
    y = (jnp.dot(om_ref[...], wo_ref[0:w, :], preferred_element_type=F32)
         + jnp.dot(od_ref[...], wo_ref[w:2 * w, :], preferred_element_type=F32))
    x1 = _layer_norm(alpha * x_ref[...] + y, g_ref[...], b_ref[...])
    x1_ref[...] = x1
    logits = jnp.dot(x1.astype(BF16), wr_ref[...], preferred_element_type=F32) + br_ref[...]
    lane = lax.broadcasted_iota(jnp.int32, (1, LANES), 1).astype(F32)
    none = float(LANES)
    gl = jnp.where(lane < N_GROUPS, logits, -jnp.inf)
    gmax = jnp.max(gl, axis=1, keepdims=True)
    gidx = jnp.min(jnp.where(gl == gmax, lane, none), axis=1, keepdims=True)
    g_w = 1.0 / jnp.sum(jnp.exp(gl - gmax), axis=1, keepdims=True)
    lo = N_GROUPS + EXPERTS_PER_GROUP * gidx
    el = jnp.where(lane >= lo, jnp.where(lane < lo + EXPERTS_PER_GROUP, logits, -jnp.inf), -jnp.inf)
    e1 = jnp.max(el, axis=1, keepdims=True)
    i1 = jnp.min(jnp.where(el == e1, lane, none), axis=1, keepdims=True)
    el2 = jnp.where(lane == i1, -jnp.inf, el)
    e2 = jnp.max(el2, axis=1, keepdims=True)
    i2 = jnp.min(jnp.where(el2 == e2, lane, none), axis=1, keepdims=True)
    t2 = jnp.exp(e2 - e1)
    w1 = g_w / (1.0 + t2)
    gates_ref[...] = jnp.where(lane == i1, w1, 0.0) + jnp.where(lane == i2, w1 * t2, 0.0)


def _outproj(x2, om, od, wo_b, ln_g, ln_b, wr_b, br, alpha):
    N, D = x2.shape
    tm = min(N, 256)
    w = GROUP_WIDTH
    full = lambda a: pl.BlockSpec(a.shape, lambda t: (0, 0))
    return pl.pallas_call(
        functools.partial(_outproj_kernel, alpha=alpha),
        grid=(N // tm,),
        in_specs=[pl.BlockSpec((tm, D), lambda t: (t, 0)),
                  pl.BlockSpec((tm, w), lambda t: (t, 0)),
                  pl.BlockSpec((tm, w), lambda t: (t, 0)),
                  full(wo_b), full(ln_g), full(ln_b), full(wr_b), full(br)],
        out_specs=[pl.BlockSpec((tm, D), lambda t: (t, 0)), pl.BlockSpec((tm, LANES), lambda t: (t, 0))],
        out_shape=[jax.ShapeDtypeStruct((N, D), F32), jax.ShapeDtypeStruct((N, LANES), F32)],
        compiler_params=_params(("parallel",)),
        name="outproj_router",
    )(x2, om, od, wo_b, ln_g, ln_b, wr_b, br)


def _moe_kernel(x1_ref, gates_ref, wg_ref, wu_ref, wd_ref, g_ref, b_ref, o_ref, xb_sc, acc_sc, *, alpha):
    e = pl.program_id(1)

    @pl.when(e == 0)
    def _():
        xb_sc[...] = x1_ref[...].astype(BF16)
        acc_sc[...] = jnp.zeros_like(acc_sc)

    xb = xb_sc[...]
    a = jnp.dot(xb, wg_ref[0], preferred_element_type=F32)
    u = jnp.dot(xb, wu_ref[0], preferred_element_type=F32)
    lane = lax.broadcasted_iota(jnp.int32, (1, LANES), 1)
    gate = jnp.sum(jnp.where(lane == e + N_GROUPS, gates_ref[...], 0.0), axis=1, keepdims=True)
    hid = (a / (1.0 + jnp.exp(-a))) * u * gate
    acc_sc[...] += jnp.dot(hid.astype(BF16), wd_ref[0], preferred_element_type=F32)

    @pl.when(e == N_EXPERTS - 1)
    def _():
        o_ref[...] = _layer_norm(alpha * x1_ref[...] + acc_sc[...], g_ref[...], b_ref[...])


def _moe(x1, gates, wg_b, wu_b, wd_b, ln_g, ln_b, alpha):
    N, D = x1.shape
    tm = min(N, 1024)
    f = wg_b.shape[2]
    full = lambda a: pl.BlockSpec(a.shape, lambda t, e: (0, 0))
    return pl.pallas_call(
        functools.partial(_moe_kernel, alpha=alpha),
        grid=(N // tm, N_EXPERTS),
        in_specs=[pl.BlockSpec((tm, D), lambda t, e: (t, 0)),
                  pl.BlockSpec((tm, LANES), lambda t, e: (t, 0)),
                  pl.BlockSpec((1, D, f), lambda t, e: (e, 0, 0)),
                  pl.BlockSpec((1, D, f), lambda t, e: (e, 0, 0)),
                  pl.BlockSpec((1, f, D), lambda t, e: (e, 0, 0)),
                  full(ln_g), full(ln_b)],
        out_specs=pl.BlockSpec((tm, D), lambda t, e: (t, 0)),
        out_shape=jax.ShapeDtypeStruct((N, D), F32),
        scratch_shapes=[pltpu.VMEM((tm, D), BF16), pltpu.VMEM((tm, D), F32)],
        compiler_params=_params(("parallel", "arbitrary")),
        name="moe_ffn",
    )(x1, gates, wg_b, wu_b, wd_b, ln_g, ln_b)


def _finish(x, om, od, lw, alpha):
    B, S, D = x.shape
    w = GROUP_WIDTH
    x1, gates = _outproj(x.reshape(B * S, D), om.reshape(B * S, w), od.reshape(B * S, w),
                         lw["wo"], lw["ln1_g"], lw["ln1_b"], lw["wr"], lw["br"], alpha)
    y = _moe(x1, gates, lw["wg"], lw["wu"], lw["wd"], lw["ln2_g"], lw["ln2_b"], alpha)
    return y.reshape(B, S, D)


def kernel(x_prompt, x_sample, cache_moba_k, cache_moba_v, cache_diff_k, cache_diff_v, page_table, w_in, w_out, g_moba, g_diff, lam_q1, lam_k1, lam_q2, lam_k2, ln1_g, ln1_b, w_group, b_group, w_expert, b_expert, w_gate, w_up, w_down, ln2_g, ln2_b):
    depth = w_in.shape[0]
    B, S, D = x_prompt.shape
    nseq, dec_seq, _ = x_sample.shape
    assert dec_seq == 1 and S % MOBA_BLOCK == 0
    n_phys, page = cache_moba_k.shape[1:3]
    past_len = page_table.shape[1] * page
    assert past_len % MOBA_BLOCK == 0 and MOBA_BLOCK % page == 0
    w = GROUP_WIDTH
    alpha = (2 * depth) ** 0.25
    pos_p = jnp.arange(S)
    pos_s = jnp.full((nseq,), past_len, jnp.int32)

    xp, xs = x_prompt, x_sample.reshape(1, nseq, D)
    outs = [[] for _ in range(8)]
    for l in range(depth):
        li = _lambda_init(l)
        w_in_b = w_in[l].astype(BF16)
        wvt_b = jnp.concatenate([w_in_b[:, 2 * w:3 * w], w_in_b[:, 5 * w:6 * w]], axis=1).T
        lam_p = jnp.stack([lam_q1[l], lam_k1[l], lam_q2[l], lam_k2[l]]).astype(F32)
        pad = jnp.zeros((D, LANES - N_GROUPS - N_EXPERTS), F32)
        lw = dict(
            wo=w_out[l].astype(BF16), ln1_g=ln1_g[l].reshape(1, D), ln1_b=ln1_b[l].reshape(1, D),
            wr=jnp.concatenate([w_group[l], w_expert[l], pad], axis=1).astype(BF16),
            br=jnp.concatenate([b_group[l], b_expert[l], pad[0]]).reshape(1, LANES),
            wg=w_gate[l].astype(BF16), wu=w_up[l].astype(BF16), wd=w_down[l].astype(BF16),
            ln2_g=ln2_g[l].reshape(1, D), ln2_b=ln2_b[l].reshape(1, D))
        gm, gd = g_moba[l].reshape(-1), g_diff[l].reshape(-1)

        qm_b, qd_b, km, vm, kd, vd, km_b, kd_b, vmt_b, vdt_b, kmean = _proj(xp, pos_p, w_in_b, wvt_b, prompt=True)
        om = _moba_prompt(qm_b, km_b, vmt_b, kmean, gm)
        od = _diff_prompt(lam_p, qd_b, kd_b, vdt_b, gd, li)
        xp = _finish(xp, om, od, lw, alpha)
        for dst, a in zip(outs[:4], (km, vm, kd, vd)):
            dst.append(a)

        qm2, qd2, km2, vm2, kd2, vd2 = _proj(xs, pos_s, w_in_b, wvt_b, prompt=False)
        caches = [c[l].reshape(n_phys, page, w) for c in (cache_moba_k, cache_moba_v, cache_diff_k, cache_diff_v)]
        per_seq = [a.reshape(nseq, 1, w) for a in (qm2, qd2, km2, vm2, kd2, vd2)]
        om2, od2 = _decode_attention(page_table, lam_p, *per_seq, caches, gm, gd, li)
        xs = _finish(xs, om2.reshape(1, nseq, w), od2.reshape(1, nseq, w), lw, alpha)
        for dst, a in zip(outs[4:], (km2, vm2, kd2, vd2)):
            dst.append(a)

    def stack(parts, tail, lead):
        return jnp.stack(parts).reshape((depth,) + lead + tail)

    mh, dh = (MOBA_HEADS, HEAD_DIM), (DIFF_HEADS, 2, DIFF_QK_DIM)
    dv = (DIFF_HEADS, DIFF_V_DIM)
    return (xp, xs.reshape(nseq, 1, D),
            stack(outs[0], mh, (B, S)), stack(outs[1], mh, (B, S)), stack(outs[2], dh, (B, S)), stack(outs[3], dv, (B, S)),
            stack(outs[4], mh, (nseq, 1)), stack(outs[5], mh, (nseq, 1)), stack(outs[6], dh, (nseq, 1)),
            stack(outs[7], dv, (nseq, 1)))
```

```python
import functools
import math

import jax
import jax.numpy as jnp
from jax import lax
from jax.experimental import pallas as pl
from jax.experimental.pallas import tpu as pltpu

F32 = jnp.float32
BF16 = jnp.bfloat16

LANES = 128
HEAD_DIM = 64
MOBA_HEADS = 8
MOBA_BLOCK = 256
MOBA_TOPK = 3
DIFF_HEADS = 4
DIFF_QK_DIM = 64
DIFF_V_DIM = 128
GROUP_WIDTH = 512
ROT_DIM = HEAD_DIM // 4
ROT_HALF = ROT_DIM // 2
ROPE_THETA = 500000.0
N_GROUPS = 4
EXPERTS_PER_GROUP = 4
N_EXPERTS = N_GROUPS * EXPERTS_PER_GROUP
EXPERT_TOPK = 2
LN_EPS = 1e-5
RMS_EPS = 1e-5
QK_SCALE = HEAD_DIM ** -0.5
MASK_BIAS = 2.0 ** 30
ONES_ROWS = 16
KV_UNROLL = 4
VMEM_LIMIT = 48 * 1024 * 1024


def _lambda_init(layer):
    return 0.8 - 0.6 * math.exp(-0.3 * layer)


def _params(sem, vmem=VMEM_LIMIT):
    return pltpu.CompilerParams(dimension_semantics=sem, vmem_limit_bytes=vmem)


def _nt_dot(a, b):
    return lax.dot_general(a, b, (((1,), (1,)), ((), ())), preferred_element_type=F32)


def _layer_norm(z, g, b):
    mu = jnp.mean(z, axis=-1, keepdims=True)
    zc = z - mu
    var = jnp.mean(zc * zc, axis=-1, keepdims=True)
    return zc * lax.rsqrt(var + LN_EPS) * g + b


def _rope_tables(pos):
    n = pos.shape[0]
    inv_freq = ROPE_THETA ** (-jnp.arange(ROT_HALF, dtype=F32) / ROT_HALF)
    ang = pos.astype(F32)[:, None] * inv_freq[None, :]
    cos, sin = jnp.cos(ang), jnp.sin(ang)
    rest = HEAD_DIM - ROT_DIM
    c64 = jnp.concatenate([cos, cos, jnp.ones((n, rest), F32)], axis=1)
    sa64 = jnp.concatenate([-sin, jnp.zeros((n, rest + ROT_HALF), F32)], axis=1)
    sb64 = jnp.concatenate([jnp.zeros((n, ROT_HALF), F32), sin, jnp.zeros((n, rest), F32)], axis=1)
    reps = LANES // HEAD_DIM
    return jnp.tile(c64, (1, reps)), jnp.tile(sa64, (1, reps)), jnp.tile(sb64, (1, reps))


def _rope(h, cos, sa, sb):
    parts = []
    for a in range(h.shape[1] // LANES):
        ha = h[:, a * LANES:(a + 1) * LANES]
        parts.append(ha * cos + pltpu.roll(ha, LANES - ROT_HALF, 1) * sa + pltpu.roll(ha, ROT_HALF, 1) * sb)
    return jnp.concatenate(parts, axis=1)


def _proj_kernel(x_ref, w_ref, wvt_ref, cos_ref, sa_ref, sb_ref, *out_refs, prompt):
    if prompt:
        (qm_ref, qd_ref, km_ref, vm_ref, kd_ref, vd_ref,
         kmb_ref, kdb_ref, vmt_ref, vdt_ref, kmean_ref) = out_refs
    else:
        qm_ref, qd_ref, km_ref, vm_ref, kd_ref, vd_ref = out_refs
    xb = x_ref[0].astype(BF16)
    cos, sa, sb = cos_ref[...], sa_ref[...], sb_ref[...]
    w = GROUP_WIDTH

    def col(c):
        return jnp.dot(xb, w_ref[:, c * w:(c + 1) * w], preferred_element_type=F32)

    qm_ref[0] = (_rope(col(0), cos, sa, sb) * QK_SCALE).astype(BF16)
    km = _rope(col(1), cos, sa, sb)
    km_ref[0] = km
    vm_ref[0] = col(2)
    qd_ref[0] = (_rope(col(3), cos, sa, sb) * QK_SCALE).astype(BF16)
    kd = _rope(col(4), cos, sa, sb)
    kd_ref[0] = kd
    vd_ref[0] = col(5)
    if prompt:
        kmb_ref[0] = km.astype(BF16)
        kdb_ref[0] = kd.astype(BF16)
        kmean_ref[0, 0] = jnp.mean(km, axis=0, keepdims=True)
        vmt_ref[0, 0] = _nt_dot(wvt_ref[0:w, :], xb).astype(BF16)
        vdt_ref[0, 0] = _nt_dot(wvt_ref[w:2 * w, :], xb).astype(BF16)


def _proj(x, pos, w_in_b, wvt_b, *, prompt):
    B, S, D = x.shape
    tm = min(S, MOBA_BLOCK)
    nt = S // tm
    w = GROUP_WIDTH
    cos, sa, sb = _rope_tables(pos)
    row_spec = pl.BlockSpec((1, tm, w), lambda b, t: (b, t, 0))
    tab_spec = pl.BlockSpec((tm, LANES), lambda b, t: (t, 0))
    out_shape = [jax.ShapeDtypeStruct((B, S, w), BF16)] * 2 + [jax.ShapeDtypeStruct((B, S, w), F32)] * 4
    out_specs = [row_spec] * 6
    if prompt:
        vt_spec = pl.BlockSpec((1, 1, w, tm), lambda b, t: (b, t, 0, 0))
        out_shape += [jax.ShapeDtypeStruct((B, S, w), BF16)] * 2
        out_shape += [jax.ShapeDtypeStruct((B, nt, w, tm), BF16)] * 2
        out_shape += [jax.ShapeDtypeStruct((B, nt, 1, w), F32)]
        out_specs += [row_spec] * 2 + [vt_spec] * 2 + [pl.BlockSpec((1, 1, 1, w), lambda b, t: (b, t, 0, 0))]
    return pl.pallas_call(
        functools.partial(_proj_kernel, prompt=prompt),
        grid=(B, nt),
        in_specs=[pl.BlockSpec((1, tm, D), lambda b, t: (b, t, 0)),
                  pl.BlockSpec(w_in_b.shape, lambda b, t: (0, 0)),
                  pl.BlockSpec(wvt_b.shape, lambda b, t: (0, 0)),
                  tab_spec, tab_spec, tab_spec],
        out_specs=out_specs,
        out_shape=out_shape,
        compiler_params=_params(("parallel", "parallel")),
        name="proj_prompt" if prompt else "proj_sample",
    )(x, w_in_b, wvt_b, cos, sa, sb)


def _attend_blocks(carry, qs, ks, vts, limits):
    if limits is not None:
        shape = (ks[0].shape[0], qs[0].shape[0])
        diff = lax.broadcasted_iota(jnp.int32, shape, 0) - lax.broadcasted_iota(jnp.int32, shape, 1)
    pairs = [(c, b) for c in range(len(qs)) for b in range(len(ks))]
    scores = {cb: _nt_dot(ks[cb[1]], qs[cb[0]]) for cb in pairs}
    maxes, probs = {}, {}
    for cb in pairs:
        s = scores[cb]
        if limits is not None:
            s = jnp.where(diff <= limits[cb[1]], s, -jnp.inf)
        mb = jnp.max(s, axis=0, keepdims=True)
        shift = mb if limits is None else jnp.where(mb == -jnp.inf, 0.0, mb)
        probs[cb] = jnp.exp(s - shift).astype(BF16)
        maxes[cb] = mb
    parts = {}
    for c, b in pairs:
        vt = vts[c][b]
        lhs = jnp.concatenate([vt, jnp.ones((ONES_ROWS, vt.shape[1]), BF16)], axis=0)
        parts[c, b] = jnp.dot(lhs, probs[c, b], preferred_element_type=F32)
    out = []
    for c, (m, acc) in enumerate(carry):
        m_new = functools.reduce(jnp.maximum, [maxes[c, b] for b in range(len(ks))], m)
        acc = acc * jnp.exp(m - m_new)
        for b in range(len(ks)):
            acc = acc + parts[c, b] * jnp.exp(maxes[c, b] - m_new)
        out.append((m_new, acc))
    return tuple(out)


def _flash(i, nb, tq, vdim, qs, load_k, load_v):
    nchain = len(qs)
    init = tuple((jnp.full((1, tq), -jnp.inf, F32), jnp.zeros((vdim + ONES_ROWS, tq), F32)) for _ in range(nchain))

    def group(carry, j0, masked):
        js = [j0 + b for b in range(KV_UNROLL)]
        jl = [jnp.minimum(j, nb - 1) for j in js] if masked else js
        ks = [load_k(j) for j in jl]
        vts = [[load_v(c, j) for j in jl] for c in range(nchain)]
        limits = [(i - j) * MOBA_BLOCK for j in js] if masked else None
        return _attend_blocks(carry, qs, ks, vts, limits)

    nfull = i // KV_UNROLL
    carry = lax.fori_loop(0, nfull, lambda g, c: group(c, g * KV_UNROLL, False), init)
    carry = group(carry, nfull * KV_UNROLL, True)
    return [acc[:vdim] / acc[vdim:vdim + 1] for _, acc in carry]


def _topk_rows(g, k):
    row = lax.broadcasted_iota(jnp.int32, g.shape, 0).astype(F32)
    none = float(g.shape[0])
    sel = jnp.zeros(g.shape, F32)
    for _ in range(k):
        mx = jnp.max(g, axis=0, keepdims=True)
        idx = jnp.min(jnp.where(g == mx, row, none), axis=0, keepdims=True)
        idx = jnp.where(mx > -jnp.inf, idx, none)
        pick = row == idx
        sel = jnp.where(pick, 1.0, sel)
        g = jnp.where(pick, -jnp.inf, g)
    return sel


def _moba_kernel(q_ref, k_ref, vt_ref, kmean_ref, g_ref, o_ref):
    i = pl.program_id(2)
    nb = k_ref.shape[1]
    tq = q_ref.shape[1]
    q = q_ref[0]
    lane = lax.broadcasted_iota(jnp.int32, (1, LANES), 1)
    kmean = kmean_ref[0].astype(BF16)
    brow = lax.broadcasted_iota(jnp.int32, (nb, tq), 0)
    qs = []
    for h in range(LANES // HEAD_DIM):
        qh = jnp.where((lane >> 6) == h, q, jnp.zeros_like(q))
        gate = _nt_dot(kmean, qh)
        sel = _topk_rows(jnp.where(brow < i, gate, -jnp.inf), MOBA_TOPK)
        bias_t = jnp.where(brow == i, 0.0, jnp.where(sel > 0.0, 0.0, -MASK_BIAS))
        if nb < LANES:
            bias_t = jnp.concatenate([bias_t, jnp.zeros((LANES - nb, tq), F32)], axis=0)
        qs.append(jnp.concatenate([qh, bias_t.T.astype(BF16)], axis=1))

    def load_k(j):
        onehot = jnp.where(lane == j, 1.0, 0.0).astype(BF16)
        return jnp.concatenate([k_ref[0, j], jnp.broadcast_to(onehot, (MOBA_BLOCK, LANES))], axis=1)

    def load_v(h, j):
        return vt_ref[0, j, h * HEAD_DIM:(h + 1) * HEAD_DIM, :]

    outs = []
    for o in _flash(i, nb, tq, HEAD_DIM, qs, load_k, load_v):
        ms = jnp.mean(o * o, axis=0, keepdims=True)
        outs.append(o * lax.rsqrt(ms + RMS_EPS))
    o_ref[0] = (jnp.concatenate(outs, axis=0).T * g_ref[...]).astype(BF16)


def _moba_prompt(qm_b, km_b, vmt_b, kmean, g_moba):
    B, S, w = qm_b.shape
    nb = S // MOBA_BLOCK
    npair = w // LANES
    k4 = km_b.reshape(B, nb, MOBA_BLOCK, w)
    return pl.pallas_call(
        _moba_kernel,
        grid=(B, npair, nb),
        in_specs=[pl.BlockSpec((1, MOBA_BLOCK, LANES), lambda b, p, i: (b, i, p)),
                  pl.BlockSpec((1, nb, MOBA_BLOCK, LANES), lambda b, p, i: (b, 0, 0, p)),
                  pl.BlockSpec((1, nb, LANES, MOBA_BLOCK), lambda b, p, i: (b, 0, p, 0)),
                  pl.BlockSpec((1, nb, LANES), lambda b, p, i: (b, 0, p)),
                  pl.BlockSpec((1, LANES), lambda b, p, i: (0, p))],
        out_specs=pl.BlockSpec((1, MOBA_BLOCK, LANES), lambda b, p, i: (b, i, p)),
        out_shape=jax.ShapeDtypeStruct((B, S, w), BF16),
        compiler_params=_params(("parallel", "parallel", "arbitrary")),
        name="moba_prompt",
    )(qm_b, k4, vmt_b, kmean.reshape(B, nb, w), g_moba.reshape(1, w))


def _lambda_value(lam_ref, lam_init):
    lp = lam_ref[...]
    a = jnp.sum(lp[0:1] * lp[1:2], axis=1, keepdims=True)
    b = jnp.sum(lp[2:3] * lp[3:4], axis=1, keepdims=True)
    return jnp.exp(a) - jnp.exp(b) + lam_init


def _diff_kernel(lam_ref, q_ref, k_ref, vt_ref, g_ref, o_ref, *, lam_init):
    i = pl.program_id(2)
    nb = k_ref.shape[1]
    tq = q_ref.shape[1]
    lam = _lambda_value(lam_ref, lam_init)
    q = q_ref[0]
    lane = lax.broadcasted_iota(jnp.int32, (1, LANES), 1)
    qs = [jnp.where((lane >> 6) == r, q, jnp.zeros_like(q)) for r in range(2)]
    outs = _flash(i, nb, tq, DIFF_V_DIM, qs, lambda j: k_ref[0, j], lambda r, j: vt_ref[0, j])
    o = outs[0] - lam * outs[1]
    ms = jnp.mean(o * o, axis=0, keepdims=True)
    o = o * lax.rsqrt(ms + RMS_EPS)
    o_ref[0] = ((o.T * g_ref[...]) * (1.0 - lam_init)).astype(BF16)


def _diff_prompt(lam_p, qd_b, kd_b, vdt_b, g_diff, lam_init):
    B, S, w = qd_b.shape
    nb = S // MOBA_BLOCK
    tq = MOBA_BLOCK
    k4 = kd_b.reshape(B, nb, tq, w)
    return pl.pallas_call(
        functools.partial(_diff_kernel, lam_init=lam_init),
        grid=(B, DIFF_HEADS, nb),
        in_specs=[pl.BlockSpec(lam_p.shape, lambda b, h, i: (0, 0)),
                  pl.BlockSpec((1, tq, LANES), lambda b, h, i: (b, i, h)),
                  pl.BlockSpec((1, nb, tq, LANES), lambda b, h, i: (b, 0, 0, h)),
                  pl.BlockSpec((1, nb, DIFF_V_DIM, tq), lambda b, h, i: (b, 0, h, 0)),
                  pl.BlockSpec((1, LANES), lambda b, h, i: (0, h))],
        out_specs=pl.BlockSpec((1, tq, LANES), lambda b, h, i: (b, i, h)),
        out_shape=jax.ShapeDtypeStruct((B, S, w), BF16),
        compiler_params=_params(("parallel", "parallel", "arbitrary")),
        name="diff_prompt",
    )(lam_p, qd_b, k4, vdt_b, g_diff.reshape(1, w))


SROWS = 16


def _block_partial(q_ref, kt_refs, v_refs, v_by_head):
    r = lax.broadcasted_iota(jnp.int32, (SROWS, GROUP_WIDTH), 0)
    c = lax.broadcasted_iota(jnp.int32, (SROWS, GROUP_WIDTH), 1)
    qrow = jnp.broadcast_to(q_ref[0].astype(F32), (SROWS, GROUP_WIDTH))
    qmat = jnp.where((c >> 6) == r, qrow, 0.0).astype(BF16)
    s = jnp.concatenate([jnp.dot(qmat, kr[0].astype(BF16), preferred_element_type=F32) for kr in kt_refs], axis=1)
    m = jnp.max(s, axis=1, keepdims=True)
    p = jnp.exp(s - m)
    pb = p.astype(BF16)
    page = kt_refs[0].shape[2]
    acc = None
    for a, vr in enumerate(v_refs):
        pa = pb[:, a * page:(a + 1) * page]
        if v_by_head:
            part = jnp.concatenate(
                [jnp.dot(pa, vr[0, pl.ds(h, page, stride=DIFF_HEADS), :].astype(BF16), preferred_element_type=F32)
                 for h in range(DIFF_HEADS)], axis=1)
        else:
            part = _nt_dot(pa, vr[0].astype(BF16))
        acc = part if acc is None else acc + part
    return jnp.sum(s, axis=1, keepdims=True), m, jnp.sum(p, axis=1, keepdims=True), acc


def _topk_list(gs, k):
    none = float(len(gs))
    gs = list(gs)
    sel = [jnp.zeros(g.shape, F32) for g in gs]
    for _ in range(k):
        mx = functools.reduce(jnp.maximum, gs)
        idx = functools.reduce(jnp.minimum, [jnp.where(g == mx, float(b), none) for b, g in enumerate(gs)])
        idx = jnp.where(mx > -jnp.inf, idx, none)
        for b in range(len(gs)):
            pick = idx == float(b)
            sel[b] = jnp.where(pick, 1.0, sel[b])
            gs[b] = jnp.where(pick, -jnp.inf, gs[b])
    return sel


def _merge_partials(sel, m_sc, l_sc, r_sc, s_new, v_new, nblk):
    m = s_new
    for j in range(nblk):
        mj = m_sc[j][:, 0:1]
        m = jnp.maximum(m, mj if sel is None else jnp.where(sel[j] > 0.0, mj, -jnp.inf))
    w_new = jnp.exp(s_new - m)
    l = w_new
    racc = w_new * v_new
    for j in range(nblk):
        wj = jnp.exp(m_sc[j][:, 0:1] - m)
        if sel is not None:
            wj = jnp.where(sel[j] > 0.0, wj, 0.0)
        l = l + wj * l_sc[j][:, 0:1]
        racc = racc + wj * r_sc[j]
    return racc / l


def _decode_kernel(pt_ref, lam_ref, qm_ref, qd_ref, kmn_ref, vmn_ref, kdn_ref, vdn_ref, gm_ref, gd_ref, *rest,
                   lam_init, ppb, nblk):
    del pt_ref
    pages = rest[:4 * ppb]
    om_ref, od_ref = rest[4 * ppb:4 * ppb + 2]
    gm_sc, mm_sc, lm_sc, rm_sc, md_sc, ld_sc, rd_sc = rest[4 * ppb + 2:]
    mk, mv, dk, dv = (pages[a * ppb:(a + 1) * ppb] for a in range(4))
    j = pl.program_id(1)

    g, m, l, r = _block_partial(qm_ref, mk, mv, False)
    gm_sc[j] = jnp.broadcast_to(g, (SROWS, LANES))
    mm_sc[j] = jnp.broadcast_to(m, (SROWS, LANES))
    lm_sc[j] = jnp.broadcast_to(l, (SROWS, LANES))
    rm_sc[j] = r
    _, m, l, r = _block_partial(qd_ref, dk, dv, True)
    md_sc[j] = jnp.broadcast_to(m, (SROWS, LANES))
    ld_sc[j] = jnp.broadcast_to(l, (SROWS, LANES))
    rd_sc[j] = r

    @pl.when(j == nblk - 1)
    def _():
        r16 = lax.broadcasted_iota(jnp.int32, (SROWS, GROUP_WIDTH), 0)
        c16 = lax.broadcasted_iota(jnp.int32, (SROWS, GROUP_WIDTH), 1)

        def new_score(q_ref, k_ref):
            qf = jnp.where((c16 >> 6) == r16, jnp.broadcast_to(q_ref[0].astype(F32), (SROWS, GROUP_WIDTH)), 0.0)
            return jnp.sum(qf * k_ref[0], axis=1, keepdims=True)

        sel = _topk_list([gm_sc[b][:, 0:1] for b in range(nblk)], MOBA_TOPK)
        o = _merge_partials(sel, mm_sc, lm_sc, rm_sc, new_score(qm_ref, kmn_ref), vmn_ref[0], nblk)
        own = (c16 >> 6) == r16
        ms = jnp.sum(jnp.where(own, o * o, 0.0), axis=1, keepdims=True) * (1.0 / HEAD_DIM)
        o = jnp.where(own, o * lax.rsqrt(ms + RMS_EPS), 0.0)
        om_ref[0] = (jnp.sum(o, axis=0, keepdims=True) * gm_ref[...]).astype(BF16)

        lam = _lambda_value(lam_ref, lam_init)
        o = _merge_partials(None, md_sc, ld_sc, rd_sc, new_score(qd_ref, kdn_ref), vdn_ref[0], nblk)
        signed = jnp.where((r16 & 1) == 0, o, -lam * o)
        o = jnp.sum(jnp.where((c16 >> 7) == (r16 >> 1), signed, 0.0), axis=0, keepdims=True)
        segs = []
        for h in range(DIFF_HEADS):
            seg = o[:, h * DIFF_V_DIM:(h + 1) * DIFF_V_DIM]
            ms = jnp.mean(seg * seg, axis=1, keepdims=True)
            segs.append(seg * lax.rsqrt(ms + RMS_EPS))
        od_ref[0] = ((jnp.concatenate(segs, axis=1) * gd_ref[...]) * (1.0 - lam_init)).astype(BF16)


def _decode_attention(page_table, lam_p, qm_b, qd_b, km, vm, kd, vd, caches, g_moba, g_diff, lam_init):
    nseq, npages = page_table.shape
    page = caches[0].shape[2]
    ppb = MOBA_BLOCK // page
    nblk = npages // ppb
    w = GROUP_WIDTH
    row = pl.BlockSpec((1, 1, w), lambda b, j, pt: (b, 0, 0))
    gain = pl.BlockSpec((1, w), lambda b, j, pt: (0, 0))

    def page_spec(shape, a):
        return pl.BlockSpec((1,) + shape[1:], lambda b, j, pt, a=a: (pt[b * npages + j * ppb + a], 0, 0))

    page_specs, page_args = [], []
    for cache in caches:
        for a in range(ppb):
            page_specs.append(page_spec(cache.shape, a))
            page_args.append(cache)
    stat = pltpu.VMEM((nblk, SROWS, LANES), F32)
    part = pltpu.VMEM((nblk, SROWS, w), F32)
    return pl.pallas_call(
        functools.partial(_decode_kernel, lam_init=lam_init, ppb=ppb, nblk=nblk),
        grid_spec=pltpu.PrefetchScalarGridSpec(
            num_scalar_prefetch=1,
            grid=(nseq, nblk),
            in_specs=[pl.BlockSpec(lam_p.shape, lambda b, j, pt: (0, 0)), row, row, row, row, row, row, gain, gain]
            + page_specs,
            out_specs=[row, row],
            scratch_shapes=[stat, stat, stat, part, stat, stat, part],
        ),
        out_shape=[jax.ShapeDtypeStruct((nseq, 1, w), BF16)] * 2,
        compiler_params=_params(("parallel", "arbitrary")),
        name="decode_attention",
    )(page_table.reshape(-1), lam_p, qm_b, qd_b, km, vm, kd, vd, g_moba.reshape(1, w), g_diff.reshape(1, w),
      *page_args)


def _outproj_kernel(x_ref, om_ref, od_ref, wo_ref, g_ref, b_ref, wr_ref, br_ref, x1_ref, gates_ref, *, alpha):
    w = GROUP_WIDTH
    y = (jnp.dot(om_ref[...], wo_ref[0:w, :], preferred_element_type=F32)
         + jnp.dot(od_ref[...], wo_ref[w:2 * w, :], preferred_element_type=F32))
    x1 = _layer_norm(alpha * x_ref[...] + y, g_ref[...], b_ref[...])
    x1_ref[...] = x1
    logits = jnp.dot(x1.astype(BF16), wr_ref[...], preferred_element_type=F32) + br_ref[...]
    lane = lax.broadcasted_iota(jnp.int32, (1, LANES), 1).astype(F32)
    none = float(LANES)
    gl = jnp.where(lane < N_GROUPS, logits, -jnp.inf)
    gmax = jnp.max(gl, axis=1, keepdims=True)
    gidx = jnp.min(jnp.where(gl == gmax, lane, none), axis=1, keepdims=True)
    g_w = 1.0 / jnp.sum(jnp.exp(gl - gmax), axis=1, keepdims=True)
    lo = N_GROUPS + EXPERTS_PER_GROUP * gidx
    el = jnp.where(lane >= lo, jnp.where(lane < lo + EXPERTS_PER_GROUP, logits, -jnp.inf), -jnp.inf)
    e1 = jnp.max(el, axis=1, keepdims=True)
    i1 = jnp.min(jnp.where(el == e1, lane, none), axis=1, keepdims=True)
    el2 = jnp.where(lane == i1, -jnp.inf, el)
    e2 = jnp.max(el2, axis=1, keepdims=True)
    i2 = jnp.min(jnp.where(el2 == e2, lane, none), axis=1, keepdims=True)
    t2 = jnp.exp(e2 - e1)
    w1 = g_w / (1.0 + t2)
    gates_ref[...] = jnp.where(lane == i1, w1, 0.0) + jnp.where(lane == i2, w1 * t2, 0.0)


def _outproj(x2, om, od, wo_b, ln_g, ln_b, wr_b, br, alpha):
    N, D = x2.shape
    tm = min(N, 256)
    w = GROUP_WIDTH
    full = lambda a: pl.BlockSpec(a.shape, lambda t: (0, 0))
    return pl.pallas_call(
        functools.partial(_outproj_kernel, alpha=alpha),
        grid=(N // tm,),
        in_specs=[pl.BlockSpec((tm, D), lambda t: (t, 0)),
                  pl.BlockSpec((tm, w), lambda t: (t, 0)),
                  pl.BlockSpec((tm, w), lambda t: (t, 0)),
                  full(wo_b), full(ln_g), full(ln_b), full(wr_b), full(br)],
        out_specs=[pl.BlockSpec((tm, D), lambda t: (t, 0)), pl.BlockSpec((tm, LANES), lambda t: (t, 0))],
        out_shape=[jax.ShapeDtypeStruct((N, D), F32), jax.ShapeDtypeStruct((N, LANES), F32)],
        compiler_params=_params(("parallel",)),
        name="outproj_router",
    )(x2, om, od, wo_b, ln_g, ln_b, wr_b, br)


def _moe_kernel(x1_ref, gates_ref, wg_ref, wu_ref, wd_ref, g_ref, b_ref, o_ref, xb_sc, acc_sc, *, alpha):
    e = pl.program_id(1)

    @pl.when(e == 0)
    def _():
        xb_sc[...] = x1_ref[...].astype(BF16)
        acc_sc[...] = jnp.zeros_like(acc_sc)

    xb = xb_sc[...]
    a = jnp.dot(xb, wg_ref[0], preferred_element_type=F32)
    u = jnp.dot(xb, wu_ref[0], preferred_element_type=F32)
    lane = lax.broadcasted_iota(jnp.int32, (1, LANES), 1)
    gate = jnp.sum(jnp.where(lane == e + N_GROUPS, gates_ref[...], 0.0), axis=1, keepdims=True)
    hid = (a / (1.0 + jnp.exp(-a))) * u * gate
    acc_sc[...] += jnp.dot(hid.astype(BF16), wd_ref[0], preferred_element_type=F32)

    @pl.when(e == N_EXPERTS - 1)
    def _():
        o_ref[...] = _layer_norm(alpha * x1_ref[...] + acc_sc[...], g_ref[...], b_ref[...])


def _moe(x1, gates, wg_b, wu_b, wd_b, ln_g, ln_b, alpha):
    N, D = x1.shape
    tm = min(N, 1024)
    f = wg_b.shape[2]
    full = lambda a: pl.BlockSpec(a.shape, lambda t, e: (0, 0))
    return pl.pallas_call(
        functools.partial(_moe_kernel, alpha=alpha),
        grid=(N // tm, N_EXPERTS),
        in_specs=[pl.BlockSpec((tm, D), lambda t, e: (t, 0)),
                  pl.BlockSpec((tm, LANES), lambda t, e: (t, 0)),
                  pl.BlockSpec((1, D, f), lambda t, e: (e, 0, 0)),
                  pl.BlockSpec((1, D, f), lambda t, e: (e, 0, 0)),
                  pl.BlockSpec((1, f, D), lambda t, e: (e, 0, 0)),
                  full(ln_g), full(ln_b)],
        out_specs=pl.BlockSpec((tm, D), lambda t, e: (t, 0)),
        out_shape=jax.ShapeDtypeStruct((N, D), F32),
        scratch_shapes=[pltpu.VMEM((tm, D), BF16), pltpu.VMEM((tm, D), F32)],
        compiler_params=_params(("parallel", "arbitrary")),
        name="moe_ffn",
    )(x1, gates, wg_b, wu_b, wd_b, ln_g, ln_b)


def _finish(x, om, od, lw, alpha):
    B, S, D = x.shape
    w = GROUP_WIDTH
    x1, gates = _outproj(x.reshape(B * S, D), om.reshape(B * S, w), od.reshape(B * S, w),
                         lw["wo"], lw["ln1_g"], lw["ln1_b"], lw["wr"], lw["br"], alpha)
    y = _moe(x1, gates, lw["wg"], lw["wu"], lw["wd"], lw["ln2_g"], lw["ln2_b"], alpha)
    return y.reshape(B, S, D)


def kernel(x_prompt, x_sample, cache_moba_k, cache_moba_v, cache_diff_k, cache_diff_v, page_table, w_in, w_out, g_moba, g_diff, lam_q1, lam_k1, lam_q2, lam_k2, ln1_g, ln1_b, w_group, b_group, w_expert, b_expert, w_gate, w_up, w_down, ln2_g, ln2_b):
    depth = w_in.shape[0]
    B, S, D = x_prompt.shape
    nseq, dec_seq, _ = x_sample.shape
    assert dec_seq == 1 and S % MOBA_BLOCK == 0
    n_phys, page = cache_moba_k.shape[1:3]
    past_len = page_table.shape[1] * page
    assert past_len % MOBA_BLOCK == 0 and MOBA_BLOCK % page == 0
    w = GROUP_WIDTH
    alpha = (2 * depth) ** 0.25
    pos_p = jnp.arange(S)
    pos_s = jnp.full((nseq,), past_len, jnp.int32)

    xp, xs = x_prompt, x_sample.reshape(1, nseq, D)
    outs = [[] for _ in range(8)]
    for l in range(depth):
        li = _lambda_init(l)
        w_in_b = w_in[l].astype(BF16)
        wvt_b = jnp.concatenate([w_in_b[:, 2 * w:3 * w], w_in_b[:, 5 * w:6 * w]], axis=1).T
        lam_p = jnp.stack([lam_q1[l], lam_k1[l], lam_q2[l], lam_k2[l]]).astype(F32)
        pad = jnp.zeros((D, LANES - N_GROUPS - N_EXPERTS), F32)
        lw = dict(
            wo=w_out[l].astype(BF16), ln1_g=ln1_g[l].reshape(1, D), ln1_b=ln1_b[l].reshape(1, D),
            wr=jnp.concatenate([w_group[l], w_expert[l], pad], axis=1).astype(BF16),
            br=jnp.concatenate([b_group[l], b_expert[l], pad[0]]).reshape(1, LANES),
            wg=w_gate[l].astype(BF16), wu=w_up[l].astype(BF16), wd=w_down[l].astype(BF16),
            ln2_g=ln2_g[l].reshape(1, D), ln2_b=ln2_b[l].reshape(1, D))
        gm, gd = g_moba[l].reshape(-1), g_diff[l].reshape(-1)

        qm_b, qd_b, km, vm, kd, vd, km_b, kd_b, vmt_b, vdt_b, kmean = _proj(xp, pos_p, w_in_b, wvt_b, prompt=True)
        om = _moba_prompt(qm_b, km_b, vmt_b, kmean, gm)
        od = _diff_prompt(lam_p, qd_b, kd_b, vdt_b, gd, li)
        xp = _finish(xp, om, od, lw, alpha)
        for dst, a in zip(outs[:4], (km, vm, kd, vd)):
            dst.append(a)

        qm2, qd2, km2, vm2, kd2, vd2 = _proj(xs, pos_s, w_in_b, wvt_b, prompt=False)
        caches = [jnp.swapaxes(c[l].reshape(n_phys, page, w), 1, 2)
                  for c in (cache_moba_k, cache_moba_v, cache_diff_k)]
        caches.append(cache_diff_v[l].reshape(n_phys, page * DIFF_HEADS, DIFF_V_DIM))
        per_seq = [a.reshape(nseq, 1, w) for a in (qm2, qd2, km2, vm2, kd2, vd2)]
        om2, od2 = _decode_attention(page_table, lam_p, *per_seq, caches, gm, gd, li)
        xs = _finish(xs, om2.reshape(1, nseq, w), od2.reshape(1, nseq, w), lw, alpha)
        for dst, a in zip(outs[4:], (km2, vm2, kd2, vd2)):
            dst.append(a)

    def stack(parts, tail, lead):
        return jnp.stack(parts).reshape((depth,) + lead + tail)

    mh, dh = (MOBA_HEADS, HEAD_DIM), (DIFF_HEADS, 2, DIFF_QK_DIM)
    dv = (DIFF_HEADS, DIFF_V_DIM)
    return (xp, xs.reshape(nseq, 1, D),
            stack(outs[0], mh, (B, S)), stack(outs[1], mh, (B, S)), stack(outs[2], dh, (B, S)), stack(outs[3], dv, (B, S)),
            stack(outs[4], mh, (nseq, 1)), stack(outs[5], mh, (nseq, 1)), stack(outs[6], dh, (nseq, 1)),
            stack(outs[7], dv, (nseq, 1)))
```

```python
import functools
import math

import jax
import jax.numpy as jnp
from jax import lax
from jax.experimental import pallas as pl
from jax.experimental.pallas import tpu as pltpu

F32 = jnp.float32
BF16 = jnp.bfloat16

LANES = 128
HEAD_DIM = 64
MOBA_HEADS = 8
MOBA_BLOCK = 256
MOBA_TOPK = 3
DIFF_HEADS = 4
DIFF_QK_DIM = 64
DIFF_V_DIM = 128
GROUP_WIDTH = 512
ROT_DIM = HEAD_DIM // 4
ROT_HALF = ROT_DIM // 2
ROPE_THETA = 500000.0
N_GROUPS = 4
EXPERTS_PER_GROUP = 4
N_EXPERTS = N_GROUPS * EXPERTS_PER_GROUP
EXPERT_TOPK = 2
LN_EPS = 1e-5
RMS_EPS = 1e-5
QK_SCALE = HEAD_DIM ** -0.5 * math.log2(math.e)
MASK_BIAS = 2.0 ** 30
ONES_ROWS = 16
KV_UNROLL = 4
VMEM_LIMIT = 48 * 1024 * 1024


def _lambda_init(layer):
    return 0.8 - 0.6 * math.exp(-0.3 * layer)


def _params(sem, vmem=VMEM_LIMIT):
    return pltpu.CompilerParams(dimension_semantics=sem, vmem_limit_bytes=vmem)


def _nt_dot(a, b):
    return lax.dot_general(a, b, (((1,), (1,)), ((), ())), preferred_element_type=F32)


def _layer_norm(z, g, b):
    mu = jnp.mean(z, axis=-1, keepdims=True)
    zc = z - mu
    var = jnp.mean(zc * zc, axis=-1, keepdims=True)
    return zc * lax.rsqrt(var + LN_EPS) * g + b


def _rope_tables(pos):
    n = pos.shape[0]
    inv_freq = ROPE_THETA ** (-jnp.arange(ROT_HALF, dtype=F32) / ROT_HALF)
    ang = pos.astype(F32)[:, None] * inv_freq[None, :]
    cos, sin = jnp.cos(ang), jnp.sin(ang)
    rest = HEAD_DIM - ROT_DIM
    c64 = jnp.concatenate([cos, cos, jnp.ones((n, rest), F32)], axis=1)
    sa64 = jnp.concatenate([-sin, jnp.zeros((n, rest + ROT_HALF), F32)], axis=1)
    sb64 = jnp.concatenate([jnp.zeros((n, ROT_HALF), F32), sin, jnp.zeros((n, rest), F32)], axis=1)
    reps = LANES // HEAD_DIM
    return jnp.tile(c64, (1, reps)), jnp.tile(sa64, (1, reps)), jnp.tile(sb64, (1, reps))


def _rope(h, cos, sa, sb):
    parts = []
    for a in range(h.shape[1] // LANES):
        ha = h[:, a * LANES:(a + 1) * LANES]
        parts.append(ha * cos + pltpu.roll(ha, LANES - ROT_HALF, 1) * sa + pltpu.roll(ha, ROT_HALF, 1) * sb)
    return jnp.concatenate(parts, axis=1)


def _proj_kernel(x_ref, w_ref, wvt_ref, cos_ref, sa_ref, sb_ref, *out_refs, prompt):
    if prompt:
        (qm_ref, qd_ref, km_ref, vm_ref, kd_ref, vd_ref,
         kmb_ref, kdb_ref, vmt_ref, vdt_ref, kmean_ref) = out_refs
    else:
        qm_ref, qd_ref, km_ref, vm_ref, kd_ref, vd_ref = out_refs
    xb = x_ref[0].astype(BF16)
    cos, sa, sb = cos_ref[...], sa_ref[...], sb_ref[...]
    w = GROUP_WIDTH

    def col(c):
        return jnp.dot(xb, w_ref[:, c * w:(c + 1) * w], preferred_element_type=F32)

    qm_ref[0] = (_rope(col(0), cos, sa, sb) * QK_SCALE).astype(BF16)
    km = _rope(col(1), cos, sa, sb)
    km_ref[0] = km
    vm_ref[0] = col(2)
    qd_ref[0] = (_rope(col(3), cos, sa, sb) * QK_SCALE).astype(BF16)
    kd = _rope(col(4), cos, sa, sb)
    kd_ref[0] = kd
    vd_ref[0] = col(5)
    if prompt:
        kmb_ref[0] = km.astype(BF16)
        kdb_ref[0] = kd.astype(BF16)
        kmean_ref[0, 0] = jnp.mean(km, axis=0, keepdims=True)
        vmt_ref[0, 0] = _nt_dot(wvt_ref[0:w, :], xb).astype(BF16)
        vdt_ref[0, 0] = _nt_dot(wvt_ref[w:2 * w, :], xb).astype(BF16)


def _proj(x, pos, w_in_b, wvt_b, *, prompt):
    B, S, D = x.shape
    tm = min(S, MOBA_BLOCK)
    nt = S // tm
    w = GROUP_WIDTH
    cos, sa, sb = _rope_tables(pos)
    row_spec = pl.BlockSpec((1, tm, w), lambda b, t: (b, t, 0))
    tab_spec = pl.BlockSpec((tm, LANES), lambda b, t: (t, 0))
    out_shape = [jax.ShapeDtypeStruct((B, S, w), BF16)] * 2 + [jax.ShapeDtypeStruct((B, S, w), F32)] * 4
    out_specs = [row_spec] * 6
    if prompt:
        vt_spec = pl.BlockSpec((1, 1, w, tm), lambda b, t: (b, t, 0, 0))
        out_shape += [jax.ShapeDtypeStruct((B, S, w), BF16)] * 2
        out_shape += [jax.ShapeDtypeStruct((B, nt, w, tm), BF16)] * 2
        out_shape += [jax.ShapeDtypeStruct((B, nt, 1, w), F32)]
        out_specs += [row_spec] * 2 + [vt_spec] * 2 + [pl.BlockSpec((1, 1, 1, w), lambda b, t: (b, t, 0, 0))]
    return pl.pallas_call(
        functools.partial(_proj_kernel, prompt=prompt),
        grid=(B, nt),
        in_specs=[pl.BlockSpec((1, tm, D), lambda b, t: (b, t, 0)),
                  pl.BlockSpec(w_in_b.shape, lambda b, t: (0, 0)),
                  pl.BlockSpec(wvt_b.shape, lambda b, t: (0, 0)),
                  tab_spec, tab_spec, tab_spec],
        out_specs=out_specs,
        out_shape=out_shape,
        compiler_params=_params(("parallel", "parallel")),
        name="proj_prompt" if prompt else "proj_sample",
    )(x, w_in_b, wvt_b, cos, sa, sb)


def _group_scores(qs, ks):
    return tuple(tuple(_nt_dot(kp, q) for kp in kc) for q, kc in zip(qs, ks))


def _group_update(carry, scores, vts, limits):
    nblk = len(scores[0])
    pairs = [(c, b) for c in range(len(scores)) for b in range(nblk)]
    if limits is not None:
        shape = scores[0][0].shape
        diff = lax.broadcasted_iota(jnp.int32, shape, 0) - lax.broadcasted_iota(jnp.int32, shape, 1)
    maxes, probs = {}, {}
    for c, b in pairs:
        s = scores[c][b]
        if limits is not None:
            s = jnp.where(diff <= limits[b], s, -jnp.inf)
        mb = jnp.max(s, axis=0, keepdims=True)
        shift = mb if limits is None else jnp.where(mb == -jnp.inf, 0.0, mb)
        probs[c, b] = jnp.exp2(s - shift).astype(BF16)
        maxes[c, b] = mb
    parts = {}
    for c, b in pairs:
        vt = vts[c][b]
        lhs = jnp.concatenate([vt, jnp.ones((ONES_ROWS, vt.shape[1]), BF16)], axis=0)
        parts[c, b] = jnp.dot(lhs, probs[c, b], preferred_element_type=F32)
    out = []
    for c, (m, acc) in enumerate(carry):
        m_new = functools.reduce(jnp.maximum, [maxes[c, b] for b in range(nblk)], m)
        acc = acc * jnp.exp2(m - m_new)
        for b in range(nblk):
            acc = acc + parts[c, b] * jnp.exp2(maxes[c, b] - m_new)
        out.append((m_new, acc))
    return tuple(out)


def _flash(i, nb, tq, vdims, qs, keys_of, values_of):
    init = tuple((jnp.full((1, tq), -jnp.inf, F32), jnp.zeros((e + ONES_ROWS, tq), F32)) for e in vdims)

    def group(carry, g, limits):
        js = [jnp.minimum(g * KV_UNROLL + b, nb - 1) for b in range(KV_UNROLL)]
        return _group_update(carry, _group_scores(qs, keys_of(js)), values_of(js), limits)

    nfull = i // KV_UNROLL
    carry = lax.fori_loop(0, nfull, lambda g, c: group(c, g, None), init)
    limits = [(i - nfull * KV_UNROLL - b) * MOBA_BLOCK for b in range(KV_UNROLL)]
    carry = group(carry, nfull, limits)
    return [acc[:e] / acc[e:e + 1] for (_, acc), e in zip(carry, vdims)]


def _topk_rows(g, k):
    row = lax.broadcasted_iota(jnp.int32, g.shape, 0).astype(F32)
    none = float(g.shape[0])
    sel = jnp.zeros(g.shape, F32)
    for _ in range(k):
        mx = jnp.max(g, axis=0, keepdims=True)
        idx = jnp.min(jnp.where(g == mx, row, none), axis=0, keepdims=True)
        idx = jnp.where(mx > -jnp.inf, idx, none)
        pick = row == idx
        sel = jnp.where(pick, 1.0, sel)
        g = jnp.where(pick, -jnp.inf, g)
    return sel


def _lambda_value(lam_ref, lam_init):
    lp = lam_ref[...]
    a = jnp.sum(lp[0:1] * lp[1:2], axis=1, keepdims=True)
    b = jnp.sum(lp[2:3] * lp[3:4], axis=1, keepdims=True)
    return jnp.exp(a) - jnp.exp(b) + lam_init


def _prompt_attn_kernel(lam_ref, qm_ref, km_ref, vmt_ref, kmean_ref, gm_ref, qd_ref, kd_ref, vdt_ref, gd_ref,
                        om_ref, od_ref, *, lam_init):
    i = pl.program_id(2)
    nb = km_ref.shape[1]
    tq = qm_ref.shape[1]
    lane = lax.broadcasted_iota(jnp.int32, (1, LANES), 1)

    qm = qm_ref[0]
    kmean = kmean_ref[0].astype(BF16)
    brow = lax.broadcasted_iota(jnp.int32, (nb, tq), 0)
    qs = []
    for h in range(LANES // HEAD_DIM):
        qh = jnp.where((lane >> 6) == h, qm, jnp.zeros_like(qm))
        gate = _nt_dot(kmean, qh)
        sel = _topk_rows(jnp.where(brow < i, gate, -jnp.inf), MOBA_TOPK)
        bias_t = jnp.where(brow == i, 0.0, jnp.where(sel > 0.0, 0.0, -MASK_BIAS))
        if nb < LANES:
            bias_t = jnp.concatenate([bias_t, jnp.zeros((LANES - nb, tq), F32)], axis=0)
        qs.append(jnp.concatenate([qh, bias_t.T.astype(BF16)], axis=1))
    qd = qd_ref[0]
    qs += [jnp.where((lane >> 6) == r, qd, jnp.zeros_like(qd)) for r in range(2)]

    def moba_keys(j):
        onehot = jnp.where(lane == j, 1.0, 0.0).astype(BF16)
        return jnp.concatenate([km_ref[0, j], jnp.broadcast_to(onehot, (MOBA_BLOCK, LANES))], axis=1)

    def keys_of(js):
        mk = [moba_keys(j) for j in js]
        dk = [kd_ref[0, j] for j in js]
        return [mk, mk, dk, dk]

    def values_of(js):
        dv = [vdt_ref[0, j] for j in js]
        return [[vmt_ref[0, j, h * HEAD_DIM:(h + 1) * HEAD_DIM, :] for j in js] for h in range(2)] + [dv, dv]

    o0, o1, d0, d1 = _flash(i, nb, tq, (HEAD_DIM, HEAD_DIM, DIFF_V_DIM, DIFF_V_DIM), qs, keys_of, values_of)

    outs = []
    for o in (o0, o1):
        ms = jnp.mean(o * o, axis=0, keepdims=True)
        outs.append(o * lax.rsqrt(ms + RMS_EPS))
    om_ref[0] = (jnp.concatenate(outs, axis=0).T * gm_ref[...]).astype(BF16)

    o = d0 - _lambda_value(lam_ref, lam_init) * d1
    ms = jnp.mean(o * o, axis=0, keepdims=True)
    o = o * lax.rsqrt(ms + RMS_EPS)
    od_ref[0] = ((o.T * gd_ref[...]) * (1.0 - lam_init)).astype(BF16)


def _prompt_attention(lam_p, qm_b, km_b, vmt_b, kmean, g_moba, qd_b, kd_b, vdt_b, g_diff, lam_init):
    B, S, w = qm_b.shape
    nb = S // MOBA_BLOCK
    assert nb % KV_UNROLL == 0 and w // LANES == DIFF_HEADS
    tq = MOBA_BLOCK
    q_spec = pl.BlockSpec((1, tq, LANES), lambda b, p, i: (b, i, p))
    k_spec = pl.BlockSpec((1, nb, tq, LANES), lambda b, p, i: (b, 0, 0, p))
    vt_spec = pl.BlockSpec((1, nb, LANES, tq), lambda b, p, i: (b, 0, p, 0))
    gain = pl.BlockSpec((1, LANES), lambda b, p, i: (0, p))
    return pl.pallas_call(
        functools.partial(_prompt_attn_kernel, lam_init=lam_init),
        grid=(B, DIFF_HEADS, nb),
        in_specs=[pl.BlockSpec(lam_p.shape, lambda b, p, i: (0, 0)),
                  q_spec, k_spec, vt_spec, pl.BlockSpec((1, nb, LANES), lambda b, p, i: (b, 0, p)), gain,
                  q_spec, k_spec, vt_spec, gain],
        out_specs=[q_spec, q_spec],
        out_shape=[jax.ShapeDtypeStruct((B, S, w), BF16)] * 2,
        compiler_params=_params(("parallel", "parallel", "arbitrary")),
        name="prompt_attention",
    )(lam_p, qm_b, km_b.reshape(B, nb, tq, w), vmt_b, kmean.reshape(B, nb, w), g_moba.reshape(1, w),
      qd_b, kd_b.reshape(B, nb, tq, w), vdt_b, g_diff.reshape(1, w))


SROWS = 16
DECODE_BLOCKS_PER_STEP = 2


def _block_scores(q_ref, kt_refs):
    r = lax.broadcasted_iota(jnp.int32, (SROWS, GROUP_WIDTH), 0)
    c = lax.broadcasted_iota(jnp.int32, (SROWS, GROUP_WIDTH), 1)
    qrow = jnp.broadcast_to(q_ref[0].astype(F32), (SROWS, GROUP_WIDTH))
    qmat = jnp.where((c >> 6) == r, qrow, 0.0).astype(BF16)
    return jnp.concatenate([jnp.dot(qmat, kr[0].astype(BF16), preferred_element_type=F32) for kr in kt_refs], axis=1)


def _block_values(pb, v_refs, v_by_head):
    page = pb.shape[1] // len(v_refs)
    acc = None
    for a, vr in enumerate(v_refs):
        pa = pb[:, a * page:(a + 1) * page]
        if v_by_head:
            part = jnp.concatenate(
                [jnp.dot(pa, vr[0, pl.ds(h, page, stride=DIFF_HEADS), :].astype(BF16), preferred_element_type=F32)
                 for h in range(DIFF_HEADS)], axis=1)
        else:
            part = _nt_dot(pa, vr[0].astype(BF16))
        acc = part if acc is None else acc + part
    return acc


def _topk_list(gs, k):
    none = float(len(gs))
    gs = list(gs)
    sel = [jnp.zeros(g.shape, F32) for g in gs]
    for _ in range(k):
        mx = functools.reduce(jnp.maximum, gs)
        idx = functools.reduce(jnp.minimum, [jnp.where(g == mx, float(b), none) for b, g in enumerate(gs)])
        idx = jnp.where(mx > -jnp.inf, idx, none)
        for b in range(len(gs)):
            pick = idx == float(b)
            sel[b] = jnp.where(pick, 1.0, sel[b])
            gs[b] = jnp.where(pick, -jnp.inf, gs[b])
    return sel


def _merge_partials(sel, m_sc, l_sc, r_sc, s_new, v_new, nblk):
    m = s_new
    for j in range(nblk):
        mj = m_sc[j][:, 0:1]
        m = jnp.maximum(m, mj if sel is None else jnp.where(sel[j] > 0.0, mj, -jnp.inf))
    w_new = jnp.exp2(s_new - m)
    l = w_new
    racc = w_new * v_new
    for j in range(nblk):
        wj = jnp.exp2(m_sc[j][:, 0:1] - m)
        if sel is not None:
            wj = jnp.where(sel[j] > 0.0, wj, 0.0)
        l = l + wj * l_sc[j][:, 0:1]
        racc = racc + wj * r_sc[j]
    return racc / l


def _decode_kernel(pt_ref, lam_ref, qm_ref, qd_ref, kmn_ref, vmn_ref, kdn_ref, vdn_ref, gm_ref, gd_ref, *rest,
                   lam_init, ppb, bps, nblk):
    del pt_ref
    npg = ppb * bps
    pages = rest[:4 * npg]
    om_ref, od_ref = rest[4 * npg:4 * npg + 2]
    gm_sc, mm_sc, lm_sc, rm_sc, md_sc, ld_sc, rd_sc = rest[4 * npg + 2:]
    mk, mv, dk, dv = (pages[a * npg:(a + 1) * npg] for a in range(4))
    step = pl.program_id(1)

    units = [(u, diff) for u in range(bps) for diff in (False, True)]
    scores = {(u, diff): _block_scores(qd_ref if diff else qm_ref, (dk if diff else mk)[u * ppb:(u + 1) * ppb])
              for u, diff in units}
    stats = {}
    for key in units:
        s = scores[key]
        m = jnp.max(s, axis=1, keepdims=True)
        p = jnp.exp2(s - m)
        stats[key] = (jnp.sum(s, axis=1, keepdims=True), m, jnp.sum(p, axis=1, keepdims=True), p.astype(BF16))
    for u, diff in units:
        g, m, l, pb = stats[u, diff]
        r = _block_values(pb, (dv if diff else mv)[u * ppb:(u + 1) * ppb], diff)
        j = step * bps + u
        m_sc, l_sc, r_sc = (md_sc, ld_sc, rd_sc) if diff else (mm_sc, lm_sc, rm_sc)
        m_sc[j] = jnp.broadcast_to(m, (SROWS, LANES))
        l_sc[j] = jnp.broadcast_to(l, (SROWS, LANES))
        r_sc[j] = r
        if not diff:
            gm_sc[j] = jnp.broadcast_to(g, (SROWS, LANES))

    @pl.when(step == nblk // bps - 1)
    def _():
        r16 = lax.broadcasted_iota(jnp.int32, (SROWS, GROUP_WIDTH), 0)
        c16 = lax.broadcasted_iota(jnp.int32, (SROWS, GROUP_WIDTH), 1)

        def new_score(q_ref, k_ref):
            qf = jnp.where((c16 >> 6) == r16, jnp.broadcast_to(q_ref[0].astype(F32), (SROWS, GROUP_WIDTH)), 0.0)
            return jnp.sum(qf * k_ref[0], axis=1, keepdims=True)

        sel = _topk_list([gm_sc[b][:, 0:1] for b in range(nblk)], MOBA_TOPK)
        o = _merge_partials(sel, mm_sc, lm_sc, rm_sc, new_score(qm_ref, kmn_ref), vmn_ref[0], nblk)
        own = (c16 >> 6) == r16
        ms = jnp.sum(jnp.where(own, o * o, 0.0), axis=1, keepdims=True) * (1.0 / HEAD_DIM)
        o = jnp.where(own, o * lax.rsqrt(ms + RMS_EPS), 0.0)
        om_ref[0] = (jnp.sum(o, axis=0, keepdims=True) * gm_ref[...]).astype(BF16)

        lam = _lambda_value(lam_ref, lam_init)
        o = _merge_partials(None, md_sc, ld_sc, rd_sc, new_score(qd_ref, kdn_ref), vdn_ref[0], nblk)
        signed = jnp.where((r16 & 1) == 0, o, -lam * o)
        o = jnp.sum(jnp.where((c16 >> 7) == (r16 >> 1), signed, 0.0), axis=0, keepdims=True)
        segs = []
        for h in range(DIFF_HEADS):
            seg = o[:, h * DIFF_V_DIM:(h + 1) * DIFF_V_DIM]
            ms = jnp.mean(seg * seg, axis=1, keepdims=True)
            segs.append(seg * lax.rsqrt(ms + RMS_EPS))
        od_ref[0] = ((jnp.concatenate(segs, axis=1) * gd_ref[...]) * (1.0 - lam_init)).astype(BF16)


def _decode_attention(page_table, lam_p, qm_b, qd_b, km, vm, kd, vd, caches, g_moba, g_diff, lam_init):
    nseq, npages = page_table.shape
    page = caches[0].shape[2]
    ppb = MOBA_BLOCK // page
    nblk = npages // ppb
    bps = DECODE_BLOCKS_PER_STEP if nblk % DECODE_BLOCKS_PER_STEP == 0 else 1
    npg = ppb * bps
    w = GROUP_WIDTH
    row = pl.BlockSpec((1, 1, w), lambda b, j, pt: (b, 0, 0))
    gain = pl.BlockSpec((1, w), lambda b, j, pt: (0, 0))

    def page_spec(shape, a):
        return pl.BlockSpec((1,) + shape[1:], lambda b, j, pt, a=a: (pt[b * npages + j * npg + a], 0, 0))

    page_specs, page_args = [], []
    for cache in caches:
        for a in range(npg):
            page_specs.append(page_spec(cache.shape, a))
            page_args.append(cache)
    stat = pltpu.VMEM((nblk, SROWS, LANES), F32)
    part = pltpu.VMEM((nblk, SROWS, w), F32)
    return pl.pallas_call(
        functools.partial(_decode_kernel, lam_init=lam_init, ppb=ppb, bps=bps, nblk=nblk),
        grid_spec=pltpu.PrefetchScalarGridSpec(
            num_scalar_prefetch=1,
            grid=(nseq, nblk // bps),
            in_specs=[pl.BlockSpec(lam_p.shape, lambda b, j, pt: (0, 0)), row, row, row, row, row, row, gain, gain]
            + page_specs,
            out_specs=[row, row],
            scratch_shapes=[stat, stat, stat, part, stat, stat, part],
        ),
        out_shape=[jax.ShapeDtypeStruct((nseq, 1, w), BF16)] * 2,
        compiler_params=_params(("parallel", "arbitrary")),
        name="decode_attention",
    )(page_table.reshape(-1), lam_p, qm_b, qd_b, km, vm, kd, vd, g_moba.reshape(1, w), g_diff.reshape(1, w),
      *page_args)


def _outproj_kernel(x_ref, om_ref, od_ref, wo_ref, g_ref, b_ref, wr_ref, br_ref, x1_ref, gates_ref, *, alpha):
    w = GROUP_WIDTH
    y = (jnp.dot(om_ref[...], wo_ref[0:w, :], preferred_element_type=F32)
         + jnp.dot(od_ref[...], wo_ref[w:2 * w, :], preferred_element_type=F32))
    x1 = _layer_norm(alpha * x_ref[...] + y, g_ref[...], b_ref[...])
    x1_ref[...] = x1
    logits = jnp.dot(x1.astype(BF16), wr_ref[...], preferred_element_type=F32) + br_ref[...]
    lane = lax.broadcasted_iota(jnp.int32, (1, LANES), 1).astype(F32)
    none = float(LANES)
    gl = jnp.where(lane < N_GROUPS, logits, -jnp.inf)
    gmax = jnp.max(gl, axis=1, keepdims=True)
    gidx = jnp.min(jnp.where(gl == gmax, lane, none), axis=1, keepdims=True)
    g_w = 1.0 / jnp.sum(jnp.exp(gl - gmax), axis=1, keepdims=True)
    lo = N_GROUPS + EXPERTS_PER_GROUP * gidx
    el = jnp.where(lane >= lo, jnp.where(lane < lo + EXPERTS_PER_GROUP, logits, -jnp.inf), -jnp.inf)
    e1 = jnp.max(el, axis=1, keepdims=True)
    i1 = jnp.min(jnp.where(el == e1, lane, none), axis=1, keepdims=True)
    el2 = jnp.where(lane == i1, -jnp.inf, el)
    e2 = jnp.max(el2, axis=1, keepdims=True)
    i2 = jnp.min(jnp.where(el2 == e2, lane, none), axis=1, keepdims=True)
    t2 = jnp.exp(e2 - e1)
    w1 = g_w / (1.0 + t2)
    gates_ref[...] = jnp.where(lane == i1, w1, 0.0) + jnp.where(lane == i2, w1 * t2, 0.0)


def _outproj(x2, om, od, wo_b, ln_g, ln_b, wr_b, br, alpha):
    N, D = x2.shape
    tm = min(N, 256)
    w = GROUP_WIDTH
    full = lambda a: pl.BlockSpec(a.shape, lambda t: (0, 0))
    return pl.pallas_call(
        functools.partial(_outproj_kernel, alpha=alpha),
        grid=(N // tm,),
        in_specs=[pl.BlockSpec((tm, D), lambda t: (t, 0)),
                  pl.BlockSpec((tm, w), lambda t: (t, 0)),
                  pl.BlockSpec((tm, w), lambda t: (t, 0)),
                  full(wo_b), full(ln_g), full(ln_b), full(wr_b), full(br)],
        out_specs=[pl.BlockSpec((tm, D), lambda t: (t, 0)), pl.BlockSpec((tm, LANES), lambda t: (t, 0))],
        out_shape=[jax.ShapeDtypeStruct((N, D), F32), jax.ShapeDtypeStruct((N, LANES), F32)],
        compiler_params=_params(("parallel",)),
        name="outproj_router",
    )(x2, om, od, wo_b, ln_g, ln_b, wr_b, br)


def _moe_kernel(x1_ref, gates_ref, wg_ref, wu_ref, wd_ref, g_ref, b_ref, o_ref, xb_sc, acc_sc, *, alpha):
    e = pl.program_id(1)

    @pl.when(e == 0)
    def _():
        xb_sc[...] = x1_ref[...].astype(BF16)
        acc_sc[...] = jnp.zeros_like(acc_sc)

    xb = xb_sc[...]
    a = jnp.dot(xb, wg_ref[0], preferred_element_type=F32)
    u = jnp.dot(xb, wu_ref[0], preferred_element_type=F32)
    lane = lax.broadcasted_iota(jnp.int32, (1, LANES), 1)
    gate = jnp.sum(jnp.where(lane == e + N_GROUPS, gates_ref[...], 0.0), axis=1, keepdims=True)
    hid = (a / (1.0 + jnp.exp(-a))) * u * gate
    acc_sc[...] += jnp.dot(hid.astype(BF16), wd_ref[0], preferred_element_type=F32)

    @pl.when(e == N_EXPERTS - 1)
    def _():
        o_ref[...] = _layer_norm(alpha * x1_ref[...] + acc_sc[...], g_ref[...], b_ref[...])


def _moe(x1, gates, wg_b, wu_b, wd_b, ln_g, ln_b, alpha):
    N, D = x1.shape
    tm = min(N, 1024)
    f = wg_b.shape[2]
    full = lambda a: pl.BlockSpec(a.shape, lambda t, e: (0, 0))
    return pl.pallas_call(
        functools.partial(_moe_kernel, alpha=alpha),
        grid=(N // tm, N_EXPERTS),
        in_specs=[pl.BlockSpec((tm, D), lambda t, e: (t, 0)),
                  pl.BlockSpec((tm, LANES), lambda t, e: (t, 0)),
                  pl.BlockSpec((1, D, f), lambda t, e: (e, 0, 0)),
                  pl.BlockSpec((1, D, f), lambda t, e: (e, 0, 0)),
                  pl.BlockSpec((1, f, D), lambda t, e: (e, 0, 0)),
                  full(ln_g), full(ln_b)],
        out_specs=pl.BlockSpec((tm, D), lambda t, e: (t, 0)),
        out_shape=jax.ShapeDtypeStruct((N, D), F32),
        scratch_shapes=[pltpu.VMEM((tm, D), BF16), pltpu.VMEM((tm, D), F32)],
        compiler_params=_params(("parallel", "arbitrary")),
        name="moe_ffn",
    )(x1, gates, wg_b, wu_b, wd_b, ln_g, ln_b)


def _finish(x, om, od, lw, alpha):
    B, S, D = x.shape
    w = GROUP_WIDTH
    x1, gates = _outproj(x.reshape(B * S, D), om.reshape(B * S, w), od.reshape(B * S, w),
                         lw["wo"], lw["ln1_g"], lw["ln1_b"], lw["wr"], lw["br"], alpha)
    y = _moe(x1, gates, lw["wg"], lw["wu"], lw["wd"], lw["ln2_g"], lw["ln2_b"], alpha)
    return y.reshape(B, S, D)


def kernel(x_prompt, x_sample, cache_moba_k, cache_moba_v, cache_diff_k, cache_diff_v, page_table, w_in, w_out, g_moba, g_diff, lam_q1, lam_k1, lam_q2, lam_k2, ln1_g, ln1_b, w_group, b_group, w_expert, b_expert, w_gate, w_up, w_down, ln2_g, ln2_b):
    depth = w_in.shape[0]
    B, S, D = x_prompt.shape
    nseq, dec_seq, _ = x_sample.shape
    assert dec_seq == 1 and S % MOBA_BLOCK == 0
    n_phys, page = cache_moba_k.shape[1:3]
    past_len = page_table.shape[1] * page
    assert past_len % MOBA_BLOCK == 0 and MOBA_BLOCK % page == 0
    w = GROUP_WIDTH
    alpha = (2 * depth) ** 0.25
    pos_p = jnp.arange(S)
    pos_s = jnp.full((nseq,), past_len, jnp.int32)

    xp, xs = x_prompt, x_sample.reshape(1, nseq, D)
    outs = [[] for _ in range(8)]
    for l in range(depth):
        li = _lambda_init(l)
        w_in_b = w_in[l].astype(BF16)
        wvt_b = jnp.concatenate([w_in_b[:, 2 * w:3 * w], w_in_b[:, 5 * w:6 * w]], axis=1).T
        lam_p = jnp.stack([lam_q1[l], lam_k1[l], lam_q2[l], lam_k2[l]]).astype(F32)
        pad = jnp.zeros((D, LANES - N_GROUPS - N_EXPERTS), F32)
        lw = dict(
            wo=w_out[l].astype(BF16), ln1_g=ln1_g[l].reshape(1, D), ln1_b=ln1_b[l].reshape(1, D),
            wr=jnp.concatenate([w_group[l], w_expert[l], pad], axis=1).astype(BF16),
            br=jnp.concatenate([b_group[l], b_expert[l], pad[0]]).reshape(1, LANES),
            wg=w_gate[l].astype(BF16), wu=w_up[l].astype(BF16), wd=w_down[l].astype(BF16),
            ln2_g=ln2_g[l].reshape(1, D), ln2_b=ln2_b[l].reshape(1, D))
        gm, gd = g_moba[l].reshape(-1), g_diff[l].reshape(-1)

        qm_b, qd_b, km, vm, kd, vd, km_b, kd_b, vmt_b, vdt_b, kmean = _proj(xp, pos_p, w_in_b, wvt_b, prompt=True)
        om, od = _prompt_attention(lam_p, qm_b, km_b, vmt_b, kmean, gm, qd_b, kd_b, vdt_b, gd, li)
        xp = _finish(xp, om, od, lw, alpha)
        for dst, a in zip(outs[:4], (km, vm, kd, vd)):
            dst.append(a)

        qm2, qd2, km2, vm2, kd2, vd2 = _proj(xs, pos_s, w_in_b, wvt_b, prompt=False)
        caches = [jnp.swapaxes(c[l].reshape(n_phys, page, w), 1, 2)
                  for c in (cache_moba_k, cache_moba_v, cache_diff_k)]
        caches.append(cache_diff_v[l].reshape(n_phys, page * DIFF_HEADS, DIFF_V_DIM))
        per_seq = [a.reshape(nseq, 1, w) for a in (qm2, qd2, km2, vm2, kd2, vd2)]
        om2, od2 = _decode_attention(page_table, lam_p, *per_seq, caches, gm, gd, li)
        xs = _finish(xs, om2.reshape(1, nseq, w), od2.reshape(1, nseq, w), lw, alpha)
        for dst, a in zip(outs[4:], (km2, vm2, kd2, vd2)):
            dst.append(a)

    def stack(parts, tail, lead):
        return jnp.stack(parts).reshape((depth,) + lead + tail)

    mh, dh = (MOBA_HEADS, HEAD_DIM), (DIFF_HEADS, 2, DIFF_QK_DIM)
    dv = (DIFF_HEADS, DIFF_V_DIM)
    return (xp, xs.reshape(nseq, 1, D),
            stack(outs[0], mh, (B, S)), stack(outs[1], mh, (B, S)), stack(outs[2], dh, (B, S)), stack(outs[3], dv, (B, S)),
            stack(outs[4], mh, (nseq, 1)), stack(outs[5], mh, (nseq, 1)), stack(outs[6], dh, (nseq, 1)),
            stack(outs[7], dv, (nseq, 1)))
```

```python
import functools
import math

import jax
import jax.numpy as jnp
from jax import lax
from jax.experimental import pallas as pl
from jax.experimental.pallas import tpu as pltpu

F32 = jnp.float32
BF16 = jnp.bfloat16

LANES = 128
HEAD_DIM = 64
MOBA_HEADS = 8
MOBA_BLOCK = 256
MOBA_TOPK = 3
DIFF_HEADS = 4
DIFF_QK_DIM = 64
DIFF_V_DIM = 128
GROUP_WIDTH = 512
ROT_DIM = HEAD_DIM // 4
ROT_HALF = ROT_DIM // 2
ROPE_THETA = 500000.0
N_GROUPS = 4
EXPERTS_PER_GROUP = 4
N_EXPERTS = N_GROUPS * EXPERTS_PER_GROUP
EXPERT_TOPK = 2
LN_EPS = 1e-5
RMS_EPS = 1e-5
QK_SCALE = HEAD_DIM ** -0.5 * math.log2(math.e)
MASK_BIAS = 2.0 ** 30
ONES_ROWS = 16
KV_UNROLL = 4
ATTN_CHAINS = 4
VMEM_LIMIT = 48 * 1024 * 1024


def _lambda_init(layer):
    return 0.8 - 0.6 * math.exp(-0.3 * layer)


def _params(sem, vmem=VMEM_LIMIT):
    return pltpu.CompilerParams(dimension_semantics=sem, vmem_limit_bytes=vmem)


def _nt_dot(a, b):
    return lax.dot_general(a, b, (((1,), (1,)), ((), ())), preferred_element_type=F32)


def _layer_norm(z, g, b):
    mu = jnp.mean(z, axis=-1, keepdims=True)
    zc = z - mu
    var = jnp.mean(zc * zc, axis=-1, keepdims=True)
    return zc * lax.rsqrt(var + LN_EPS) * g + b


def _rope_tables(pos):
    n = pos.shape[0]
    inv_freq = ROPE_THETA ** (-jnp.arange(ROT_HALF, dtype=F32) / ROT_HALF)
    ang = pos.astype(F32)[:, None] * inv_freq[None, :]
    cos, sin = jnp.cos(ang), jnp.sin(ang)
    rest = HEAD_DIM - ROT_DIM
    c64 = jnp.concatenate([cos, cos, jnp.ones((n, rest), F32)], axis=1)
    sa64 = jnp.concatenate([-sin, jnp.zeros((n, rest + ROT_HALF), F32)], axis=1)
    sb64 = jnp.concatenate([jnp.zeros((n, ROT_HALF), F32), sin, jnp.zeros((n, rest), F32)], axis=1)
    reps = LANES // HEAD_DIM
    return jnp.tile(c64, (1, reps)), jnp.tile(sa64, (1, reps)), jnp.tile(sb64, (1, reps))


def _rope(h, cos, sa, sb):
    parts = []
    for a in range(h.shape[1] // LANES):
        ha = h[:, a * LANES:(a + 1) * LANES]
        parts.append(ha * cos + pltpu.roll(ha, LANES - ROT_HALF, 1) * sa + pltpu.roll(ha, ROT_HALF, 1) * sb)
    return jnp.concatenate(parts, axis=1)


def _proj_kernel(x_ref, w_ref, wvt_ref, cos_ref, sa_ref, sb_ref, *out_refs, prompt):
    if prompt:
        (qm_ref, qd_ref, km_ref, vm_ref, kd_ref, vd_ref,
         kmb_ref, kdb_ref, vmt_ref, vdt_ref, kmean_ref) = out_refs
    else:
        qm_ref, qd_ref, km_ref, vm_ref, kd_ref, vd_ref = out_refs
    xb = x_ref[0].astype(BF16)
    cos, sa, sb = cos_ref[...], sa_ref[...], sb_ref[...]
    w = GROUP_WIDTH

    def col(c):
        return jnp.dot(xb, w_ref[:, c * w:(c + 1) * w], preferred_element_type=F32)

    qm_ref[0] = (_rope(col(0), cos, sa, sb) * QK_SCALE).astype(BF16)
    km = _rope(col(1), cos, sa, sb)
    km_ref[0] = km
    vm_ref[0] = col(2)
    qd_ref[0] = (_rope(col(3), cos, sa, sb) * QK_SCALE).astype(BF16)
    kd = _rope(col(4), cos, sa, sb)
    kd_ref[0] = kd
    vd_ref[0] = col(5)
    if prompt:
        kmb_ref[0] = km.astype(BF16)
        kdb_ref[0] = kd.astype(BF16)
        kmean_ref[0, 0] = jnp.mean(km, axis=0, keepdims=True)
        vmt_ref[0, 0] = _nt_dot(wvt_ref[0:w, :], xb).astype(BF16)
        vdt_ref[0, 0] = _nt_dot(wvt_ref[w:2 * w, :], xb).astype(BF16)


def _proj(x, pos, w_in_b, wvt_b, *, prompt):
    B, S, D = x.shape
    tm = min(S, MOBA_BLOCK)
    nt = S // tm
    w = GROUP_WIDTH
    cos, sa, sb = _rope_tables(pos)
    row_spec = pl.BlockSpec((1, tm, w), lambda b, t: (b, t, 0))
    tab_spec = pl.BlockSpec((tm, LANES), lambda b, t: (t, 0))
    out_shape = [jax.ShapeDtypeStruct((B, S, w), BF16)] * 2 + [jax.ShapeDtypeStruct((B, S, w), F32)] * 4
    out_specs = [row_spec] * 6
    if prompt:
        vt_spec = pl.BlockSpec((1, 1, w, tm), lambda b, t: (b, t, 0, 0))
        out_shape += [jax.ShapeDtypeStruct((B, S, w), BF16)] * 2
        out_shape += [jax.ShapeDtypeStruct((B, nt, w, tm), BF16)] * 2
        out_shape += [jax.ShapeDtypeStruct((B, nt, 1, w), F32)]
        out_specs += [row_spec] * 2 + [vt_spec] * 2 + [pl.BlockSpec((1, 1, 1, w), lambda b, t: (b, t, 0, 0))]
    return pl.pallas_call(
        functools.partial(_proj_kernel, prompt=prompt),
        grid=(B, nt),
        in_specs=[pl.BlockSpec((1, tm, D), lambda b, t: (b, t, 0)),
                  pl.BlockSpec(w_in_b.shape, lambda b, t: (0, 0)),
                  pl.BlockSpec(wvt_b.shape, lambda b, t: (0, 0)),
                  tab_spec, tab_spec, tab_spec],
        out_specs=out_specs,
        out_shape=out_shape,
        compiler_params=_params(("parallel", "parallel")),
        name="proj_prompt" if prompt else "proj_sample",
    )(x, w_in_b, wvt_b, cos, sa, sb)


def _group_update(carry, get_score, vts, causal_last=False, next_score=None):
    nblk = len(vts[0])
    pairs = [(c, b) for c in range(len(vts)) for b in range(nblk)]
    maxes, parts = {}, {}
    for k in range(0, len(pairs), 2):
        duo = pairs[k:k + 2]
        if next_score is not None:
            for c, b in duo:
                next_score(c, b)
        probs = {}
        for c, b in duo:
            s = get_score(c, b)
            if causal_last and b == nblk - 1:
                krow = lax.broadcasted_iota(jnp.int32, s.shape, 0)
                qcol = lax.broadcasted_iota(jnp.int32, s.shape, 1)
                s = jnp.where(krow <= qcol, s, -jnp.inf)
            mb = jnp.max(s, axis=0, keepdims=True)
            probs[c, b] = jnp.exp2(s - mb).astype(BF16)
            maxes[c, b] = mb
        for c, b in duo:
            vt = vts[c][b]
            lhs = jnp.concatenate([vt, jnp.ones((ONES_ROWS, vt.shape[1]), BF16)], axis=0)
            parts[c, b] = jnp.dot(lhs, probs[c, b], preferred_element_type=F32)
    out = []
    for c, (m, acc) in enumerate(carry):
        m_new = functools.reduce(jnp.maximum, [maxes[c, b] for b in range(nblk)], m)
        acc = acc * jnp.exp2(m - m_new)
        for b in range(nblk):
            acc = acc + parts[c, b] * jnp.exp2(maxes[c, b] - m_new)
        out.append((m_new, acc))
    return tuple(out)


def _flash(i, nb, tq, vdims, qs, keys_of, values_of, buf):
    buf_a, buf_b = buf.at[0], buf.at[1]
    nchain = len(qs)
    init = tuple((jnp.full((1, tq), -jnp.inf, F32), jnp.zeros((e + ONES_ROWS, tq), F32)) for e in vdims)

    def blocks(g):
        return [jnp.minimum(g * KV_UNROLL + b, nb - 1) for b in range(KV_UNROLL)]

    def scorer(g, dst):
        ks = keys_of(blocks(g))

        def score_tile(c, b):
            dst[c * KV_UNROLL + b] = jnp.dot(ks[c][b], qs[c](), preferred_element_type=F32)
        return score_tile

    def update(carry, src, g, dst):
        return _group_update(carry, lambda c, b: src[c * KV_UNROLL + b], values_of(blocks(g)),
                             next_score=scorer(g + 1, dst))

    def last_update(carry, src, g, nvis):
        return _group_update(carry, lambda c, b: src[c * KV_UNROLL + b], values_of(blocks(g)[:nvis]),
                             causal_last=True)

    nfull = i // KV_UNROLL

    def two_groups(t, carry):
        carry = update(carry, buf_a, 2 * t, buf_b)
        return update(carry, buf_b, 2 * t + 1, buf_a)

    first = scorer(0, buf_a)
    for c in range(nchain):
        for b in range(KV_UNROLL):
            first(c, b)
    carry = lax.fori_loop(0, nfull // 2, two_groups, init)
    odd = nfull % 2
    carry = lax.cond(odd == 1, lambda carry: update(carry, buf_a, nfull - 1, buf_b), lambda carry: carry, carry)
    src = buf.at[odd]
    carry = lax.switch(i - nfull * KV_UNROLL,
                       [functools.partial(last_update, src=src, g=nfull, nvis=n + 1) for n in range(KV_UNROLL)], carry)
    return [acc[:e] / acc[e:e + 1] for (_, acc), e in zip(carry, vdims)]


def _topk_rows(g, k):
    row = lax.broadcasted_iota(jnp.int32, g.shape, 0).astype(F32)
    none = float(g.shape[0])
    sel = jnp.zeros(g.shape, F32)
    for _ in range(k):
        mx = jnp.max(g, axis=0, keepdims=True)
        idx = jnp.min(jnp.where(g == mx, row, none), axis=0, keepdims=True)
        idx = jnp.where(mx > -jnp.inf, idx, none)
        pick = row == idx
        sel = jnp.where(pick, 1.0, sel)
        g = jnp.where(pick, -jnp.inf, g)
    return sel


def _lambda_value(lam_ref, lam_init):
    lp = lam_ref[...]
    a = jnp.sum(lp[0:1] * lp[1:2], axis=1, keepdims=True)
    b = jnp.sum(lp[2:3] * lp[3:4], axis=1, keepdims=True)
    return jnp.exp(a) - jnp.exp(b) + lam_init


def _prompt_attn_kernel(lam_ref, qm_ref, km_ref, vmt_ref, kmean_ref, gm_ref, qd_ref, kd_ref, vdt_ref, gd_ref,
                        om_ref, od_ref, buf, *, lam_init):
    i = pl.program_id(2)
    nb = km_ref.shape[1]
    tq = qm_ref.shape[1]
    lane = lax.broadcasted_iota(jnp.int32, (1, LANES), 1)

    half = lax.broadcasted_iota(jnp.int32, (LANES, tq), 0) >> 6
    qmt = qm_ref[0].astype(F32).T
    kmean = kmean_ref[0].astype(BF16)
    brow = lax.broadcasted_iota(jnp.int32, (nb, tq), 0)
    qhts = [jnp.where(half == h, qmt, 0.0).astype(BF16) for h in range(LANES // HEAD_DIM)]
    gates = [jnp.dot(kmean, qht, preferred_element_type=F32) for qht in qhts]

    def moba_query(h):
        qht = qhts[h]
        sel = _topk_rows(jnp.where(brow < i, gates[h], -jnp.inf), MOBA_TOPK)
        bias_t = jnp.where(brow == i, 0.0, jnp.where(sel > 0.0, 0.0, -MASK_BIAS))
        if nb < LANES:
            bias_t = jnp.concatenate([bias_t, jnp.zeros((LANES - nb, tq), F32)], axis=0)
        return jnp.concatenate([qht, bias_t.astype(BF16)], axis=0)

    qdt = qd_ref[0].astype(F32).T
    qs = [functools.cache(functools.partial(lambda r: jnp.where(half == r, qdt, 0.0).astype(BF16), r))
          for r in range(2)]
    qs += [functools.cache(functools.partial(moba_query, h)) for h in range(LANES // HEAD_DIM)]

    def moba_keys(j):
        onehot = jnp.where(lane == j, 1.0, 0.0).astype(BF16)
        return jnp.concatenate([km_ref[0, j], jnp.broadcast_to(onehot, (MOBA_BLOCK, LANES))], axis=1)

    def keys_of(js):
        mk = [moba_keys(j) for j in js]
        dk = [kd_ref[0, j] for j in js]
        return [dk, dk, mk, mk]

    def values_of(js):
        dv = [vdt_ref[0, j] for j in js]
        return [dv, dv] + [[vmt_ref[0, j, h * HEAD_DIM:(h + 1) * HEAD_DIM, :] for j in js] for h in range(2)]

    d0, d1, o0, o1 = _flash(i, nb, tq, (DIFF_V_DIM, DIFF_V_DIM, HEAD_DIM, HEAD_DIM), qs, keys_of, values_of,
                            buf)

    outs = []
    for o in (o0, o1):
        ms = jnp.mean(o * o, axis=0, keepdims=True)
        outs.append(o * lax.rsqrt(ms + RMS_EPS))
    om_ref[0] = (jnp.concatenate(outs, axis=0).T * gm_ref[...]).astype(BF16)

    o = d0 - _lambda_value(lam_ref, lam_init) * d1
    ms = jnp.mean(o * o, axis=0, keepdims=True)
    o = o * lax.rsqrt(ms + RMS_EPS)
    od_ref[0] = ((o.T * gd_ref[...]) * (1.0 - lam_init)).astype(BF16)


def _prompt_attention(lam_p, qm_b, km_b, vmt_b, kmean, g_moba, qd_b, kd_b, vdt_b, g_diff, lam_init):
    B, S, w = qm_b.shape
    nb = S // MOBA_BLOCK
    assert nb % KV_UNROLL == 0 and w // LANES == DIFF_HEADS
    tq = MOBA_BLOCK
    q_spec = pl.BlockSpec((1, tq, LANES), lambda b, p, i: (b, i, p))
    k_spec = pl.BlockSpec((1, nb, tq, LANES), lambda b, p, i: (b, 0, 0, p))
    vt_spec = pl.BlockSpec((1, nb, LANES, tq), lambda b, p, i: (b, 0, p, 0))
    gain = pl.BlockSpec((1, LANES), lambda b, p, i: (0, p))
    return pl.pallas_call(
        functools.partial(_prompt_attn_kernel, lam_init=lam_init),
        grid=(B, DIFF_HEADS, nb),
        in_specs=[pl.BlockSpec(lam_p.shape, lambda b, p, i: (0, 0)),
                  q_spec, k_spec, vt_spec, pl.BlockSpec((1, nb, LANES), lambda b, p, i: (b, 0, p)), gain,
                  q_spec, k_spec, vt_spec, gain],
        out_specs=[q_spec, q_spec],
        out_shape=[jax.ShapeDtypeStruct((B, S, w), BF16)] * 2,
        scratch_shapes=[pltpu.VMEM((2, ATTN_CHAINS * KV_UNROLL, tq, tq), F32)],
        compiler_params=_params(("parallel", "parallel", "arbitrary")),
        name="prompt_attention",
    )(lam_p, qm_b, km_b.reshape(B, nb, tq, w), vmt_b, kmean.reshape(B, nb, w), g_moba.reshape(1, w),
      qd_b, kd_b.reshape(B, nb, tq, w), vdt_b, g_diff.reshape(1, w))


SROWS = 16
DECODE_BLOCKS_PER_STEP = 2


def _block_scores(q_ref, kt_refs):
    r = lax.broadcasted_iota(jnp.int32, (SROWS, GROUP_WIDTH), 0)
    c = lax.broadcasted_iota(jnp.int32, (SROWS, GROUP_WIDTH), 1)
    qrow = jnp.broadcast_to(q_ref[0].astype(F32), (SROWS, GROUP_WIDTH))
    qmat = jnp.where((c >> 6) == r, qrow, 0.0).astype(BF16)
    return jnp.concatenate([jnp.dot(qmat, kr[0].astype(BF16), preferred_element_type=F32) for kr in kt_refs], axis=1)


def _block_values(pb, v_refs, v_by_head):
    page = pb.shape[1] // len(v_refs)
    acc = None
    for a, vr in enumerate(v_refs):
        pa = pb[:, a * page:(a + 1) * page]
        if v_by_head:
            part = jnp.concatenate(
                [jnp.dot(pa, vr[0, pl.ds(h, page, stride=DIFF_HEADS), :].astype(BF16), preferred_element_type=F32)
                 for h in range(DIFF_HEADS)], axis=1)
        else:
            part = _nt_dot(pa, vr[0].astype(BF16))
        acc = part if acc is None else acc + part
    return acc


def _topk_list(gs, k):
    none = float(len(gs))
    gs = list(gs)
    sel = [jnp.zeros(g.shape, F32) for g in gs]
    for _ in range(k):
        mx = functools.reduce(jnp.maximum, gs)
        idx = functools.reduce(jnp.minimum, [jnp.where(g == mx, float(b), none) for b, g in enumerate(gs)])
        idx = jnp.where(mx > -jnp.inf, idx, none)
        for b in range(len(gs)):
            pick = idx == float(b)
            sel[b] = jnp.where(pick, 1.0, sel[b])
            gs[b] = jnp.where(pick, -jnp.inf, gs[b])
    return sel


def _merge_partials(sel, m_sc, l_sc, r_sc, s_new, v_new, nblk):
    m = s_new
    for j in range(nblk):
        mj = m_sc[j][:, 0:1]
        m = jnp.maximum(m, mj if sel is None else jnp.where(sel[j] > 0.0, mj, -jnp.inf))
    w_new = jnp.exp2(s_new - m)
    l = w_new
    racc = w_new * v_new
    for j in range(nblk):
        wj = jnp.exp2(m_sc[j][:, 0:1] - m)
        if sel is not None:
            wj = jnp.where(sel[j] > 0.0, wj, 0.0)
        l = l + wj * l_sc[j][:, 0:1]
        racc = racc + wj * r_sc[j]
    return racc / l


def _decode_kernel(pt_ref, lam_ref, qm_ref, qd_ref, kmn_ref, vmn_ref, kdn_ref, vdn_ref, gm_ref, gd_ref, *rest,
                   lam_init, ppb, bps, nblk):
    del pt_ref
    npg = ppb * bps
    pages = rest[:4 * npg]
    om_ref, od_ref = rest[4 * npg:4 * npg + 2]
    gm_sc, mm_sc, lm_sc, rm_sc, md_sc, ld_sc, rd_sc = rest[4 * npg + 2:]
    mk, mv, dk, dv = (pages[a * npg:(a + 1) * npg] for a in range(4))
    step = pl.program_id(1)

    units = [(u, diff) for u in range(bps) for diff in (False, True)]
    scores = {(u, diff): _block_scores(qd_ref if diff else qm_ref, (dk if diff else mk)[u * ppb:(u + 1) * ppb])
              for u, diff in units}
    stats = {}
    for key in units:
        s = scores[key]
        m = jnp.max(s, axis=1, keepdims=True)
        p = jnp.exp2(s - m)
        stats[key] = (jnp.sum(s, axis=1, keepdims=True), m, jnp.sum(p, axis=1, keepdims=True), p.astype(BF16))
    for u, diff in units:
        g, m, l, pb = stats[u, diff]
        r = _block_values(pb, (dv if diff else mv)[u * ppb:(u + 1) * ppb], diff)
        j = step * bps + u
        m_sc, l_sc, r_sc = (md_sc, ld_sc, rd_sc) if diff else (mm_sc, lm_sc, rm_sc)
        m_sc[j] = jnp.broadcast_to(m, (SROWS, LANES))
        l_sc[j] = jnp.broadcast_to(l, (SROWS, LANES))
        r_sc[j] = r
        if not diff:
            gm_sc[j] = jnp.broadcast_to(g, (SROWS, LANES))

    @pl.when(step == nblk // bps - 1)
    def _():
        r16 = lax.broadcasted_iota(jnp.int32, (SROWS, GROUP_WIDTH), 0)
        c16 = lax.broadcasted_iota(jnp.int32, (SROWS, GROUP_WIDTH), 1)

        def new_score(q_ref, k_ref):
            qf = jnp.where((c16 >> 6) == r16, jnp.broadcast_to(q_ref[0].astype(F32), (SROWS, GROUP_WIDTH)), 0.0)
            return jnp.sum(qf * k_ref[0], axis=1, keepdims=True)

        sel = _topk_list([gm_sc[b][:, 0:1] for b in range(nblk)], MOBA_TOPK)
        o = _merge_partials(sel, mm_sc, lm_sc, rm_sc, new_score(qm_ref, kmn_ref), vmn_ref[0], nblk)
        own = (c16 >> 6) == r16
        ms = jnp.sum(jnp.where(own, o * o, 0.0), axis=1, keepdims=True) * (1.0 / HEAD_DIM)
        o = jnp.where(own, o * lax.rsqrt(ms + RMS_EPS), 0.0)
        om_ref[0] = (jnp.sum(o, axis=0, keepdims=True) * gm_ref[...]).astype(BF16)

        lam = _lambda_value(lam_ref, lam_init)
        o = _merge_partials(None, md_sc, ld_sc, rd_sc, new_score(qd_ref, kdn_ref), vdn_ref[0], nblk)
        signed = jnp.where((r16 & 1) == 0, o, -lam * o)
        o = jnp.sum(jnp.where((c16 >> 7) == (r16 >> 1), signed, 0.0), axis=0, keepdims=True)
        segs = []
        for h in range(DIFF_HEADS):
            seg = o[:, h * DIFF_V_DIM:(h + 1) * DIFF_V_DIM]
            ms = jnp.mean(seg * seg, axis=1, keepdims=True)
            segs.append(seg * lax.rsqrt(ms + RMS_EPS))
        od_ref[0] = ((jnp.concatenate(segs, axis=1) * gd_ref[...]) * (1.0 - lam_init)).astype(BF16)


def _decode_attention(page_table, lam_p, qm_b, qd_b, km, vm, kd, vd, caches, g_moba, g_diff, lam_init):
    nseq, npages = page_table.shape
    page = caches[0].shape[2]
    ppb = MOBA_BLOCK // page
    nblk = npages // ppb
    bps = DECODE_BLOCKS_PER_STEP if nblk % DECODE_BLOCKS_PER_STEP == 0 else 1
    npg = ppb * bps
    w = GROUP_WIDTH
    row = pl.BlockSpec((1, 1, w), lambda b, j, pt: (b, 0, 0))
    gain = pl.BlockSpec((1, w), lambda b, j, pt: (0, 0))

    def page_spec(shape, a):
        return pl.BlockSpec((1,) + shape[1:], lambda b, j, pt, a=a: (pt[b * npages + j * npg + a], 0, 0))

    page_specs, page_args = [], []
    for cache in caches:
        for a in range(npg):
            page_specs.append(page_spec(cache.shape, a))
            page_args.append(cache)
    stat = pltpu.VMEM((nblk, SROWS, LANES), F32)
    part = pltpu.VMEM((nblk, SROWS, w), F32)
    return pl.pallas_call(
        functools.partial(_decode_kernel, lam_init=lam_init, ppb=ppb, bps=bps, nblk=nblk),
        grid_spec=pltpu.PrefetchScalarGridSpec(
            num_scalar_prefetch=1,
            grid=(nseq, nblk // bps),
            in_specs=[pl.BlockSpec(lam_p.shape, lambda b, j, pt: (0, 0)), row, row, row, row, row, row, gain, gain]
            + page_specs,
            out_specs=[row, row],
            scratch_shapes=[stat, stat, stat, part, stat, stat, part],
        ),
        out_shape=[jax.ShapeDtypeStruct((nseq, 1, w), BF16)] * 2,
        compiler_params=_params(("parallel", "arbitrary")),
        name="decode_attention",
    )(page_table.reshape(-1), lam_p, qm_b, qd_b, km, vm, kd, vd, g_moba.reshape(1, w), g_diff.reshape(1, w),
      *page_args)


def _outproj_kernel(x_ref, om_ref, od_ref, wo_ref, g_ref, b_ref, wr_ref, br_ref, x1_ref, gates_ref, *, alpha):
    w = GROUP_WIDTH
    y = (jnp.dot(om_ref[...], wo_ref[0:w, :], preferred_element_type=F32)
         + jnp.dot(od_ref[...], wo_ref[w:2 * w, :], preferred_element_type=F32))
    x1 = _layer_norm(alpha * x_ref[...] + y, g_ref[...], b_ref[...])
    x1_ref[...] = x1
    logits = jnp.dot(x1.astype(BF16), wr_ref[...], preferred_element_type=F32) + br_ref[...]
    lane = lax.broadcasted_iota(jnp.int32, (1, LANES), 1).astype(F32)
    none = float(LANES)
    gl = jnp.where(lane < N_GROUPS, logits, -jnp.inf)
    gmax = jnp.max(gl, axis=1, keepdims=True)
    gidx = jnp.min(jnp.where(gl == gmax, lane, none), axis=1, keepdims=True)
    g_w = 1.0 / jnp.sum(jnp.exp(gl - gmax), axis=1, keepdims=True)
    lo = N_GROUPS + EXPERTS_PER_GROUP * gidx
    el = jnp.where(lane >= lo, jnp.where(lane < lo + EXPERTS_PER_GROUP, logits, -jnp.inf), -jnp.inf)
    e1 = jnp.max(el, axis=1, keepdims=True)
    i1 = jnp.min(jnp.where(el == e1, lane, none), axis=1, keepdims=True)
    el2 = jnp.where(lane == i1, -jnp.inf, el)
    e2 = jnp.max(el2, axis=1, keepdims=True)
    i2 = jnp.min(jnp.where(el2 == e2, lane, none), axis=1, keepdims=True)
    t2 = jnp.exp(e2 - e1)
    w1 = g_w / (1.0 + t2)
    gates_ref[...] = jnp.where(lane == i1, w1, 0.0) + jnp.where(lane == i2, w1 * t2, 0.0)


def _outproj(x2, om, od, wo_b, ln_g, ln_b, wr_b, br, alpha):
    N, D = x2.shape
    tm = min(N, 256)
    w = GROUP_WIDTH
    full = lambda a: pl.BlockSpec(a.shape, lambda t: (0, 0))
    return pl.pallas_call(
        functools.partial(_outproj_kernel, alpha=alpha),
        grid=(N // tm,),
        in_specs=[pl.BlockSpec((tm, D), lambda t: (t, 0)),
                  pl.BlockSpec((tm, w), lambda t: (t, 0)),
                  pl.BlockSpec((tm, w), lambda t: (t, 0)),
                  full(wo_b), full(ln_g), full(ln_b), full(wr_b), full(br)],
        out_specs=[pl.BlockSpec((tm, D), lambda t: (t, 0)), pl.BlockSpec((tm, LANES), lambda t: (t, 0))],
        out_shape=[jax.ShapeDtypeStruct((N, D), F32), jax.ShapeDtypeStruct((N, LANES), F32)],
        compiler_params=_params(("parallel",)),
        name="outproj_router",
    )(x2, om, od, wo_b, ln_g, ln_b, wr_b, br)


def _moe_kernel(x1_ref, gates_ref, wg_ref, wu_ref, wd_ref, g_ref, b_ref, o_ref, xb_sc, acc_sc, *, alpha):
    e = pl.program_id(1)

    @pl.when(e == 0)
    def _():
        xb_sc[...] = x1_ref[...].astype(BF16)
        acc_sc[...] = jnp.zeros_like(acc_sc)

    xb = xb_sc[...]
    a = jnp.dot(xb, wg_ref[0], preferred_element_type=F32)
    u = jnp.dot(xb, wu_ref[0], preferred_element_type=F32)
    lane = lax.broadcasted_iota(jnp.int32, (1, LANES), 1)
    gate = jnp.sum(jnp.where(lane == e + N_GROUPS, gates_ref[...], 0.0), axis=1, keepdims=True)
    hid = (a / (1.0 + jnp.exp(-a))) * u * gate
    acc_sc[...] += jnp.dot(hid.astype(BF16), wd_ref[0], preferred_element_type=F32)

    @pl.when(e == N_EXPERTS - 1)
    def _():
        o_ref[...] = _layer_norm(alpha * x1_ref[...] + acc_sc[...], g_ref[...], b_ref[...])


def _moe(x1, gates, wg_b, wu_b, wd_b, ln_g, ln_b, alpha):
    N, D = x1.shape
    tm = min(N, 1024)
    f = wg_b.shape[2]
    full = lambda a: pl.BlockSpec(a.shape, lambda t, e: (0, 0))
    return pl.pallas_call(
        functools.partial(_moe_kernel, alpha=alpha),
        grid=(N // tm, N_EXPERTS),
        in_specs=[pl.BlockSpec((tm, D), lambda t, e: (t, 0)),
                  pl.BlockSpec((tm, LANES), lambda t, e: (t, 0)),
                  pl.BlockSpec((1, D, f), lambda t, e: (e, 0, 0)),
                  pl.BlockSpec((1, D, f), lambda t, e: (e, 0, 0)),
                  pl.BlockSpec((1, f, D), lambda t, e: (e, 0, 0)),
                  full(ln_g), full(ln_b)],
        out_specs=pl.BlockSpec((tm, D), lambda t, e: (t, 0)),
        out_shape=jax.ShapeDtypeStruct((N, D), F32),
        scratch_shapes=[pltpu.VMEM((tm, D), BF16), pltpu.VMEM((tm, D), F32)],
        compiler_params=_params(("parallel", "arbitrary")),
        name="moe_ffn",
    )(x1, gates, wg_b, wu_b, wd_b, ln_g, ln_b)


def _finish(x, om, od, lw, alpha):
    B, S, D = x.shape
    w = GROUP_WIDTH
    x1, gates = _outproj(x.reshape(B * S, D), om.reshape(B * S, w), od.reshape(B * S, w),
                         lw["wo"], lw["ln1_g"], lw["ln1_b"], lw["wr"], lw["br"], alpha)
    y = _moe(x1, gates, lw["wg"], lw["wu"], lw["wd"], lw["ln2_g"], lw["ln2_b"], alpha)
    return y.reshape(B, S, D)


def kernel(x_prompt, x_sample, cache_moba_k, cache_moba_v, cache_diff_k, cache_diff_v, page_table, w_in, w_out, g_moba, g_diff, lam_q1, lam_k1, lam_q2, lam_k2, ln1_g, ln1_b, w_group, b_group, w_expert, b_expert, w_gate, w_up, w_down, ln2_g, ln2_b):
    depth = w_in.shape[0]
    B, S, D = x_prompt.shape
    nseq, dec_seq, _ = x_sample.shape
    assert dec_seq == 1 and S % MOBA_BLOCK == 0
    n_phys, page = cache_moba_k.shape[1:3]
    past_len = page_table.shape[1] * page
    assert past_len % MOBA_BLOCK == 0 and MOBA_BLOCK % page == 0
    w = GROUP_WIDTH
    alpha = (2 * depth) ** 0.25
    pos_p = jnp.arange(S)
    pos_s = jnp.full((nseq,), past_len, jnp.int32)

    xp, xs = x_prompt, x_sample.reshape(1, nseq, D)
    outs = [[] for _ in range(8)]
    for l in range(depth):
        li = _lambda_init(l)
        w_in_b = w_in[l].astype(BF16)
        wvt_b = jnp.concatenate([w_in_b[:, 2 * w:3 * w], w_in_b[:, 5 * w:6 * w]], axis=1).T
        lam_p = jnp.stack([lam_q1[l], lam_k1[l], lam_q2[l], lam_k2[l]]).astype(F32)
        pad = jnp.zeros((D, LANES - N_GROUPS - N_EXPERTS), F32)
        lw = dict(
            wo=w_out[l].astype(BF16), ln1_g=ln1_g[l].reshape(1, D), ln1_b=ln1_b[l].reshape(1, D),
            wr=jnp.concatenate([w_group[l], w_expert[l], pad], axis=1).astype(BF16),
            br=jnp.concatenate([b_group[l], b_expert[l], pad[0]]).reshape(1, LANES),
            wg=w_gate[l].astype(BF16), wu=w_up[l].astype(BF16), wd=w_down[l].astype(BF16),
            ln2_g=ln2_g[l].reshape(1, D), ln2_b=ln2_b[l].reshape(1, D))
        gm, gd = g_moba[l].reshape(-1), g_diff[l].reshape(-1)

        qm_b, qd_b, km, vm, kd, vd, km_b, kd_b, vmt_b, vdt_b, kmean = _proj(xp, pos_p, w_in_b, wvt_b, prompt=True)
        om, od = _prompt_attention(lam_p, qm_b, km_b, vmt_b, kmean, gm, qd_b, kd_b, vdt_b, gd, li)
        xp = _finish(xp, om, od, lw, alpha)
        for dst, a in zip(outs[:4], (km, vm, kd, vd)):
            dst.append(a)

        qm2, qd2, km2, vm2, kd2, vd2 = _proj(xs, pos_s, w_in_b, wvt_b, prompt=False)
        caches = [jnp.swapaxes(c[l].reshape(n_phys, page, w), 1, 2)
                  for c in (cache_moba_k, cache_moba_v, cache_diff_k)]
        caches.append(cache_diff_v[l].reshape(n_phys, page * DIFF_HEADS, DIFF_V_DIM))
        per_seq = [a.reshape(nseq, 1, w) for a in (qm2, qd2, km2, vm2, kd2, vd2)]
        om2, od2 = _decode_attention(page_table, lam_p, *per_seq, caches, gm, gd, li)
        xs = _finish(xs, om2.reshape(1, nseq, w), od2.reshape(1, nseq, w), lw, alpha)
        for dst, a in zip(outs[4:], (km2, vm2, kd2, vd2)):
            dst.append(a)

    def stack(parts, tail, lead):
        return jnp.stack(parts).reshape((depth,) + lead + tail)

    mh, dh = (MOBA_HEADS, HEAD_DIM), (DIFF_HEADS, 2, DIFF_QK_DIM)
    dv = (DIFF_HEADS, DIFF_V_DIM)
    return (xp, xs.reshape(nseq, 1, D),
            stack(outs[0], mh, (B, S)), stack(outs[1], mh, (B, S)), stack(outs[2], dh, (B, S)), stack(outs[3], dv, (B, S)),
            stack(outs[4], mh, (nseq, 1)), stack(outs[5], mh, (nseq, 1)), stack(outs[6], dh, (nseq, 1)),
            stack(outs[7], dv, (nseq, 1)))
```

```python
import functools
import math

import jax
import jax.numpy as jnp
from jax import lax
from jax.experimental import pallas as pl
from jax.experimental.pallas import tpu as pltpu

F32 = jnp.float32
BF16 = jnp.bfloat16

LANES = 128
HEAD_DIM = 64
MOBA_HEADS = 8
MOBA_BLOCK = 256
MOBA_TOPK = 3
DIFF_HEADS = 4
DIFF_QK_DIM = 64
DIFF_V_DIM = 128
GROUP_WIDTH = 512
ROT_DIM = HEAD_DIM // 4
ROT_HALF = ROT_DIM // 2
ROPE_THETA = 500000.0
N_GROUPS = 4
EXPERTS_PER_GROUP = 4
N_EXPERTS = N_GROUPS * EXPERTS_PER_GROUP
EXPERT_TOPK = 2
LN_EPS = 1e-5
RMS_EPS = 1e-5
QK_SCALE = HEAD_DIM ** -0.5 * math.log2(math.e)
MASK_BIAS = 2.0 ** 30
ONES_ROWS = 16
KV_UNROLL = 4
ATTN_CHAINS = 4
VMEM_LIMIT = 48 * 1024 * 1024
VMEM_LIMIT_FUSED = 56 * 1024 * 1024
MOE_TILE = 1024


def _lambda_init(layer):
    return 0.8 - 0.6 * math.exp(-0.3 * layer)


def _params(sem, vmem=VMEM_LIMIT):
    return pltpu.CompilerParams(dimension_semantics=sem, vmem_limit_bytes=vmem)


def _nt_dot(a, b):
    return lax.dot_general(a, b, (((1,), (1,)), ((), ())), preferred_element_type=F32)


def _layer_norm(z, g, b):
    mu = jnp.mean(z, axis=-1, keepdims=True)
    zc = z - mu
    var = jnp.mean(zc * zc, axis=-1, keepdims=True)
    return zc * lax.rsqrt(var + LN_EPS) * g + b


def _rope_tables(pos):
    n = pos.shape[0]
    inv_freq = ROPE_THETA ** (-jnp.arange(ROT_HALF, dtype=F32) / ROT_HALF)
    ang = pos.astype(F32)[:, None] * inv_freq[None, :]
    cos, sin = jnp.cos(ang), jnp.sin(ang)
    rest = HEAD_DIM - ROT_DIM
    c64 = jnp.concatenate([cos, cos, jnp.ones((n, rest), F32)], axis=1)
    sa64 = jnp.concatenate([-sin, jnp.zeros((n, rest + ROT_HALF), F32)], axis=1)
    sb64 = jnp.concatenate([jnp.zeros((n, ROT_HALF), F32), sin, jnp.zeros((n, rest), F32)], axis=1)
    reps = LANES // HEAD_DIM
    return jnp.tile(c64, (1, reps)), jnp.tile(sa64, (1, reps)), jnp.tile(sb64, (1, reps))


def _rope(h, cos, sa, sb):
    parts = []
    for a in range(h.shape[1] // LANES):
        ha = h[:, a * LANES:(a + 1) * LANES]
        parts.append(ha * cos + pltpu.roll(ha, LANES - ROT_HALF, 1) * sa + pltpu.roll(ha, ROT_HALF, 1) * sb)
    return jnp.concatenate(parts, axis=1)


def _proj_kernel(x_ref, w_ref, wvt_ref, cos_ref, sa_ref, sb_ref, *out_refs, prompt):
    if prompt:
        (qm_ref, qd_ref, km_ref, vm_ref, kd_ref, vd_ref,
         kmb_ref, kdb_ref, vmt_ref, vdt_ref, kmean_ref) = out_refs
    else:
        qm_ref, qd_ref, km_ref, vm_ref, kd_ref, vd_ref = out_refs
    xb = x_ref[0].astype(BF16)
    cos, sa, sb = cos_ref[...], sa_ref[...], sb_ref[...]
    w = GROUP_WIDTH

    def col(c):
        return jnp.dot(xb, w_ref[:, c * w:(c + 1) * w], preferred_element_type=F32)

    qm_ref[0] = (_rope(col(0), cos, sa, sb) * QK_SCALE).astype(BF16)
    km = _rope(col(1), cos, sa, sb)
    km_ref[0] = km
    vm_ref[0] = col(2)
    qd_ref[0] = (_rope(col(3), cos, sa, sb) * QK_SCALE).astype(BF16)
    kd = _rope(col(4), cos, sa, sb)
    kd_ref[0] = kd
    vd_ref[0] = col(5)
    if prompt:
        kmb_ref[0] = km.astype(BF16)
        kdb_ref[0] = kd.astype(BF16)
        kmean_ref[0, 0] = jnp.mean(km, axis=0, keepdims=True)
        vmt_ref[0, 0] = _nt_dot(wvt_ref[0:w, :], xb).astype(BF16)
        vdt_ref[0, 0] = _nt_dot(wvt_ref[w:2 * w, :], xb).astype(BF16)


def _proj(x, pos, w_in_b, wvt_b, *, prompt):
    B, S, D = x.shape
    tm = min(S, MOBA_BLOCK)
    nt = S // tm
    w = GROUP_WIDTH
    cos, sa, sb = _rope_tables(pos)
    row_spec = pl.BlockSpec((1, tm, w), lambda b, t: (b, t, 0))
    tab_spec = pl.BlockSpec((tm, LANES), lambda b, t: (t, 0))
    out_shape = [jax.ShapeDtypeStruct((B, S, w), BF16)] * 2 + [jax.ShapeDtypeStruct((B, S, w), F32)] * 4
    out_specs = [row_spec] * 6
    if prompt:
        vt_spec = pl.BlockSpec((1, 1, w, tm), lambda b, t: (b, t, 0, 0))
        out_shape += [jax.ShapeDtypeStruct((B, S, w), BF16)] * 2
        out_shape += [jax.ShapeDtypeStruct((B, nt, w, tm), BF16)] * 2
        out_shape += [jax.ShapeDtypeStruct((B, nt, 1, w), F32)]
        out_specs += [row_spec] * 2 + [vt_spec] * 2 + [pl.BlockSpec((1, 1, 1, w), lambda b, t: (b, t, 0, 0))]
    return pl.pallas_call(
        functools.partial(_proj_kernel, prompt=prompt),
        grid=(B, nt),
        in_specs=[pl.BlockSpec((1, tm, D), lambda b, t: (b, t, 0)),
                  pl.BlockSpec(w_in_b.shape, lambda b, t: (0, 0)),
                  pl.BlockSpec(wvt_b.shape, lambda b, t: (0, 0)),
                  tab_spec, tab_spec, tab_spec],
        out_specs=out_specs,
        out_shape=out_shape,
        compiler_params=_params(("parallel", "parallel")),
        name="proj_prompt" if prompt else "proj_sample",
    )(x, w_in_b, wvt_b, cos, sa, sb)


def _group_update(carry, get_score, vts, causal_last=False, next_score=None):
    nblk = len(vts[0])
    pairs = [(c, b) for c in range(len(vts)) for b in range(nblk)]
    maxes, parts = {}, {}
    for k in range(0, len(pairs), 2):
        duo = pairs[k:k + 2]
        if next_score is not None:
            for c, b in duo:
                next_score(c, b)
        probs = {}
        for c, b in duo:
            s = get_score(c, b)
            if causal_last and b == nblk - 1:
                krow = lax.broadcasted_iota(jnp.int32, s.shape, 0)
                qcol = lax.broadcasted_iota(jnp.int32, s.shape, 1)
                s = jnp.where(krow <= qcol, s, -jnp.inf)
            mb = jnp.max(s, axis=0, keepdims=True)
            probs[c, b] = jnp.exp2(s - mb).astype(BF16)
            maxes[c, b] = mb
        for c, b in duo:
            vt = vts[c][b]
            lhs = jnp.concatenate([vt, jnp.ones((ONES_ROWS, vt.shape[1]), BF16)], axis=0)
            parts[c, b] = jnp.dot(lhs, probs[c, b], preferred_element_type=F32)
    out = []
    for c, (m, acc) in enumerate(carry):
        m_new = functools.reduce(jnp.maximum, [maxes[c, b] for b in range(nblk)], m)
        acc = acc * jnp.exp2(m - m_new)
        for b in range(nblk):
            acc = acc + parts[c, b] * jnp.exp2(maxes[c, b] - m_new)
        out.append((m_new, acc))
    return tuple(out)


def _flash(i, nb, tq, vdims, qs, keys_of, values_of, buf):
    buf_a, buf_b = buf.at[0], buf.at[1]
    nchain = len(qs)
    init = tuple((jnp.full((1, tq), -jnp.inf, F32), jnp.zeros((e + ONES_ROWS, tq), F32)) for e in vdims)

    def blocks(g):
        return [jnp.minimum(g * KV_UNROLL + b, nb - 1) for b in range(KV_UNROLL)]

    def scorer(g, dst):
        ks = keys_of(blocks(g))

        def score_tile(c, b):
            dst[c * KV_UNROLL + b] = jnp.dot(ks[c][b], qs[c](), preferred_element_type=F32)
        return score_tile

    def update(carry, src, g, dst):
        return _group_update(carry, lambda c, b: src[c * KV_UNROLL + b], values_of(blocks(g)),
                             next_score=scorer(g + 1, dst))

    def last_update(carry, src, g, nvis):
        return _group_update(carry, lambda c, b: src[c * KV_UNROLL + b], values_of(blocks(g)[:nvis]),
                             causal_last=True)

    nfull = i // KV_UNROLL

    def two_groups(t, carry):
        carry = update(carry, buf_a, 2 * t, buf_b)
        return update(carry, buf_b, 2 * t + 1, buf_a)

    first = scorer(0, buf_a)
    for c in range(nchain):
        for b in range(KV_UNROLL):
            first(c, b)
    carry = lax.fori_loop(0, nfull // 2, two_groups, init)
    odd = nfull % 2
    carry = lax.cond(odd == 1, lambda carry: update(carry, buf_a, nfull - 1, buf_b), lambda carry: carry, carry)
    src = buf.at[odd]
    carry = lax.switch(i - nfull * KV_UNROLL,
                       [functools.partial(last_update, src=src, g=nfull, nvis=n + 1) for n in range(KV_UNROLL)], carry)
    return [acc[:e] / acc[e:e + 1] for (_, acc), e in zip(carry, vdims)]


def _topk_rows(g, k):
    row = lax.broadcasted_iota(jnp.int32, g.shape, 0).astype(F32)
    none = float(g.shape[0])
    sel = jnp.zeros(g.shape, F32)
    for _ in range(k):
        mx = jnp.max(g, axis=0, keepdims=True)
        idx = jnp.min(jnp.where(g == mx, row, none), axis=0, keepdims=True)
        idx = jnp.where(mx > -jnp.inf, idx, none)
        pick = row == idx
        sel = jnp.where(pick, 1.0, sel)
        g = jnp.where(pick, -jnp.inf, g)
    return sel


def _lambda_value(lam_ref, lam_init):
    lp = lam_ref[...]
    a = jnp.sum(lp[0:1] * lp[1:2], axis=1, keepdims=True)
    b = jnp.sum(lp[2:3] * lp[3:4], axis=1, keepdims=True)
    return jnp.exp(a) - jnp.exp(b) + lam_init


def _prompt_attn_kernel(lam_ref, qm_ref, km_ref, vmt_ref, kmean_ref, gm_ref, qd_ref, kd_ref, vdt_ref, gd_ref,
                        om_ref, od_ref, buf, *, lam_init):
    i = pl.program_id(2)
    nb = km_ref.shape[1]
    tq = qm_ref.shape[1]
    lane = lax.broadcasted_iota(jnp.int32, (1, LANES), 1)

    half = lax.broadcasted_iota(jnp.int32, (LANES, tq), 0) >> 6
    qmt = qm_ref[0].astype(F32).T
    kmean = kmean_ref[0].astype(BF16)
    brow = lax.broadcasted_iota(jnp.int32, (nb, tq), 0)
    qhts = [jnp.where(half == h, qmt, 0.0).astype(BF16) for h in range(LANES // HEAD_DIM)]
    gates = [jnp.dot(kmean, qht, preferred_element_type=F32) for qht in qhts]

    def moba_query(h):
        qht = qhts[h]
        sel = _topk_rows(jnp.where(brow < i, gates[h], -jnp.inf), MOBA_TOPK)
        bias_t = jnp.where(brow == i, 0.0, jnp.where(sel > 0.0, 0.0, -MASK_BIAS))
        if nb < LANES:
            bias_t = jnp.concatenate([bias_t, jnp.zeros((LANES - nb, tq), F32)], axis=0)
        return jnp.concatenate([qht, bias_t.astype(BF16)], axis=0)

    qdt = qd_ref[0].astype(F32).T
    qs = [functools.cache(functools.partial(lambda r: jnp.where(half == r, qdt, 0.0).astype(BF16), r))
          for r in range(2)]
    qs += [functools.cache(functools.partial(moba_query, h)) for h in range(LANES // HEAD_DIM)]

    def moba_keys(j):
        onehot = jnp.where(lane == j, 1.0, 0.0).astype(BF16)
        return jnp.concatenate([km_ref[0, j], jnp.broadcast_to(onehot, (MOBA_BLOCK, LANES))], axis=1)

    def keys_of(js):
        mk = [moba_keys(j) for j in js]
        dk = [kd_ref[0, j] for j in js]
        return [dk, dk, mk, mk]

    def values_of(js):
        dv = [vdt_ref[0, j] for j in js]
        return [dv, dv] + [[vmt_ref[0, j, h * HEAD_DIM:(h + 1) * HEAD_DIM, :] for j in js] for h in range(2)]

    d0, d1, o0, o1 = _flash(i, nb, tq, (DIFF_V_DIM, DIFF_V_DIM, HEAD_DIM, HEAD_DIM), qs, keys_of, values_of,
                            buf)

    outs = []
    for o in (o0, o1):
        ms = jnp.mean(o * o, axis=0, keepdims=True)
        outs.append(o * lax.rsqrt(ms + RMS_EPS))
    om_ref[0] = (jnp.concatenate(outs, axis=0).T * gm_ref[...]).astype(BF16)

    o = d0 - _lambda_value(lam_ref, lam_init) * d1
    ms = jnp.mean(o * o, axis=0, keepdims=True)
    o = o * lax.rsqrt(ms + RMS_EPS)
    od_ref[0] = ((o.T * gd_ref[...]) * (1.0 - lam_init)).astype(BF16)


def _prompt_attention(lam_p, qm_b, km_b, vmt_b, kmean, g_moba, qd_b, kd_b, vdt_b, g_diff, lam_init):
    B, S, w = qm_b.shape
    nb = S // MOBA_BLOCK
    assert nb % KV_UNROLL == 0 and w // LANES == DIFF_HEADS
    tq = MOBA_BLOCK
    q_spec = pl.BlockSpec((1, tq, LANES), lambda b, p, i: (b, i, p))
    k_spec = pl.BlockSpec((1, nb, tq, LANES), lambda b, p, i: (b, 0, 0, p))
    vt_spec = pl.BlockSpec((1, nb, LANES, tq), lambda b, p, i: (b, 0, p, 0))
    gain = pl.BlockSpec((1, LANES), lambda b, p, i: (0, p))
    return pl.pallas_call(
        functools.partial(_prompt_attn_kernel, lam_init=lam_init),
        grid=(B, DIFF_HEADS, nb),
        in_specs=[pl.BlockSpec(lam_p.shape, lambda b, p, i: (0, 0)),
                  q_spec, k_spec, vt_spec, pl.BlockSpec((1, nb, LANES), lambda b, p, i: (b, 0, p)), gain,
                  q_spec, k_spec, vt_spec, gain],
        out_specs=[q_spec, q_spec],
        out_shape=[jax.ShapeDtypeStruct((B, S, w), BF16)] * 2,
        scratch_shapes=[pltpu.VMEM((2, ATTN_CHAINS * KV_UNROLL, tq, tq), F32)],
        compiler_params=_params(("parallel", "parallel", "arbitrary")),
        name="prompt_attention",
    )(lam_p, qm_b, km_b.reshape(B, nb, tq, w), vmt_b, kmean.reshape(B, nb, w), g_moba.reshape(1, w),
      qd_b, kd_b.reshape(B, nb, tq, w), vdt_b, g_diff.reshape(1, w))


SROWS = 16
DECODE_BLOCKS_PER_STEP = 2


def _block_scores(q_ref, kt_refs):
    r = lax.broadcasted_iota(jnp.int32, (SROWS, GROUP_WIDTH), 0)
    c = lax.broadcasted_iota(jnp.int32, (SROWS, GROUP_WIDTH), 1)
    qrow = jnp.broadcast_to(q_ref[0].astype(F32), (SROWS, GROUP_WIDTH))
    qmat = jnp.where((c >> 6) == r, qrow, 0.0).astype(BF16)
    return jnp.concatenate([jnp.dot(qmat, kr[0].astype(BF16), preferred_element_type=F32) for kr in kt_refs], axis=1)


def _block_values(pb, v_refs, v_by_head):
    page = pb.shape[1] // len(v_refs)
    acc = None
    for a, vr in enumerate(v_refs):
        pa = pb[:, a * page:(a + 1) * page]
        if v_by_head:
            part = jnp.concatenate(
                [jnp.dot(pa, vr[0, pl.ds(h, page, stride=DIFF_HEADS), :].astype(BF16), preferred_element_type=F32)
                 for h in range(DIFF_HEADS)], axis=1)
        else:
            part = _nt_dot(pa, vr[0].astype(BF16))
        acc = part if acc is None else acc + part
    return acc


def _topk_list(gs, k):
    none = float(len(gs))
    gs = list(gs)
    sel = [jnp.zeros(g.shape, F32) for g in gs]
    for _ in range(k):
        mx = functools.reduce(jnp.maximum, gs)
        idx = functools.reduce(jnp.minimum, [jnp.where(g == mx, float(b), none) for b, g in enumerate(gs)])
        idx = jnp.where(mx > -jnp.inf, idx, none)
        for b in range(len(gs)):
            pick = idx == float(b)
            sel[b] = jnp.where(pick, 1.0, sel[b])
            gs[b] = jnp.where(pick, -jnp.inf, gs[b])
    return sel


def _merge_partials(sel, m_sc, l_sc, r_sc, s_new, v_new, nblk):
    m = s_new
    for j in range(nblk):
        mj = m_sc[j][:, 0:1]
        m = jnp.maximum(m, mj if sel is None else jnp.where(sel[j] > 0.0, mj, -jnp.inf))
    w_new = jnp.exp2(s_new - m)
    l = w_new
    racc = w_new * v_new
    for j in range(nblk):
        wj = jnp.exp2(m_sc[j][:, 0:1] - m)
        if sel is not None:
            wj = jnp.where(sel[j] > 0.0, wj, 0.0)
        l = l + wj * l_sc[j][:, 0:1]
        racc = racc + wj * r_sc[j]
    return racc / l


def _decode_partials(step, qm_ref, qd_ref, pages, stats_sc, *, ppb, bps):
    npg = ppb * bps
    gm_sc, mm_sc, lm_sc, rm_sc, md_sc, ld_sc, rd_sc = stats_sc
    mk, mv, dk, dv = (pages[a * npg:(a + 1) * npg] for a in range(4))
    units = [(u, diff) for u in range(bps) for diff in (False, True)]
    scores = {(u, diff): _block_scores(qd_ref if diff else qm_ref, (dk if diff else mk)[u * ppb:(u + 1) * ppb])
              for u, diff in units}
    stats = {}
    for key in units:
        s = scores[key]
        m = jnp.max(s, axis=1, keepdims=True)
        p = jnp.exp2(s - m)
        stats[key] = (jnp.sum(s, axis=1, keepdims=True), m, jnp.sum(p, axis=1, keepdims=True), p.astype(BF16))
    for u, diff in units:
        g, m, l, pb = stats[u, diff]
        r = _block_values(pb, (dv if diff else mv)[u * ppb:(u + 1) * ppb], diff)
        j = step * bps + u
        m_sc, l_sc, r_sc = (md_sc, ld_sc, rd_sc) if diff else (mm_sc, lm_sc, rm_sc)
        m_sc[j] = jnp.broadcast_to(m, (SROWS, LANES))
        l_sc[j] = jnp.broadcast_to(l, (SROWS, LANES))
        r_sc[j] = r
        if not diff:
            gm_sc[j] = jnp.broadcast_to(g, (SROWS, LANES))

def _decode_merge(lam_ref, qm_ref, qd_ref, kmn_ref, vmn_ref, kdn_ref, vdn_ref, gm_ref, gd_ref, om_ref, od_ref,
                  stats_sc, *, lam_init, nblk):
    gm_sc, mm_sc, lm_sc, rm_sc, md_sc, ld_sc, rd_sc = stats_sc
    r16 = lax.broadcasted_iota(jnp.int32, (SROWS, GROUP_WIDTH), 0)
    c16 = lax.broadcasted_iota(jnp.int32, (SROWS, GROUP_WIDTH), 1)

    def new_score(q_ref, k_ref):
        qf = jnp.where((c16 >> 6) == r16, jnp.broadcast_to(q_ref[0].astype(F32), (SROWS, GROUP_WIDTH)), 0.0)
        return jnp.sum(qf * k_ref[0], axis=1, keepdims=True)

    sel = _topk_list([gm_sc[b][:, 0:1] for b in range(nblk)], MOBA_TOPK)
    o = _merge_partials(sel, mm_sc, lm_sc, rm_sc, new_score(qm_ref, kmn_ref), vmn_ref[0], nblk)
    own = (c16 >> 6) == r16
    ms = jnp.sum(jnp.where(own, o * o, 0.0), axis=1, keepdims=True) * (1.0 / HEAD_DIM)
    o = jnp.where(own, o * lax.rsqrt(ms + RMS_EPS), 0.0)
    om_ref[0] = (jnp.sum(o, axis=0, keepdims=True) * gm_ref[...]).astype(BF16)

    lam = _lambda_value(lam_ref, lam_init)
    o = _merge_partials(None, md_sc, ld_sc, rd_sc, new_score(qd_ref, kdn_ref), vdn_ref[0], nblk)
    signed = jnp.where((r16 & 1) == 0, o, -lam * o)
    o = jnp.sum(jnp.where((c16 >> 7) == (r16 >> 1), signed, 0.0), axis=0, keepdims=True)
    segs = []
    for h in range(DIFF_HEADS):
        seg = o[:, h * DIFF_V_DIM:(h + 1) * DIFF_V_DIM]
        ms = jnp.mean(seg * seg, axis=1, keepdims=True)
        segs.append(seg * lax.rsqrt(ms + RMS_EPS))
    od_ref[0] = ((jnp.concatenate(segs, axis=1) * gd_ref[...]) * (1.0 - lam_init)).astype(BF16)


def _decode_kernel(pt_ref, *refs, lam_init, ppb, bps, nblk):
    del pt_ref
    npg = ppb * bps
    seq_refs, pages = refs[:9], refs[9:9 + 4 * npg]
    om_ref, od_ref = refs[9 + 4 * npg:11 + 4 * npg]
    stats_sc = refs[11 + 4 * npg:]
    step = pl.program_id(1)
    _decode_partials(step, seq_refs[1], seq_refs[2], pages, stats_sc, ppb=ppb, bps=bps)

    @pl.when(step == nblk // bps - 1)
    def _():
        _decode_merge(*seq_refs, om_ref, od_ref, stats_sc, lam_init=lam_init, nblk=nblk)


def _decode_geometry(page_table, caches):
    nseq, npages = page_table.shape
    page = caches[0].shape[2]
    ppb = MOBA_BLOCK // page
    nblk = npages // ppb
    bps = DECODE_BLOCKS_PER_STEP if nblk % DECODE_BLOCKS_PER_STEP == 0 else 1
    return nseq, npages, ppb, nblk, bps


def _decode_scratch(nblk):
    stat = pltpu.VMEM((nblk, SROWS, LANES), F32)
    part = pltpu.VMEM((nblk, SROWS, GROUP_WIDTH), F32)
    return [stat, stat, stat, part, stat, stat, part]


def _decode_attention(page_table, lam_p, qm_b, qd_b, km, vm, kd, vd, caches, g_moba, g_diff, lam_init):
    nseq, npages, ppb, nblk, bps = _decode_geometry(page_table, caches)
    npg = ppb * bps
    w = GROUP_WIDTH
    row = pl.BlockSpec((1, 1, w), lambda b, j, pt: (b, 0, 0))
    gain = pl.BlockSpec((1, w), lambda b, j, pt: (0, 0))

    def page_spec(shape, a):
        return pl.BlockSpec((1,) + shape[1:], lambda b, j, pt, a=a: (pt[b * npages + j * npg + a], 0, 0))

    page_specs, page_args = [], []
    for cache in caches:
        for a in range(npg):
            page_specs.append(page_spec(cache.shape, a))
            page_args.append(cache)
    return pl.pallas_call(
        functools.partial(_decode_kernel, lam_init=lam_init, ppb=ppb, bps=bps, nblk=nblk),
        grid_spec=pltpu.PrefetchScalarGridSpec(
            num_scalar_prefetch=1,
            grid=(nseq, nblk // bps),
            in_specs=[pl.BlockSpec(lam_p.shape, lambda b, j, pt: (0, 0)), row, row, row, row, row, row, gain, gain]
            + page_specs,
            out_specs=[row, row],
            scratch_shapes=_decode_scratch(nblk),
        ),
        out_shape=[jax.ShapeDtypeStruct((nseq, 1, w), BF16)] * 2,
        compiler_params=_params(("parallel", "arbitrary")),
        name="decode_attention",
    )(page_table.reshape(-1), lam_p, qm_b, qd_b, km, vm, kd, vd, g_moba.reshape(1, w), g_diff.reshape(1, w),
      *page_args)


def _outproj_kernel(x_ref, om_ref, od_ref, wo_ref, g_ref, b_ref, wr_ref, br_ref, x1_ref, gates_ref, *, alpha):
    w = GROUP_WIDTH
    y = (jnp.dot(om_ref[...], wo_ref[0:w, :], preferred_element_type=F32)
         + jnp.dot(od_ref[...], wo_ref[w:2 * w, :], preferred_element_type=F32))
    x1 = _layer_norm(alpha * x_ref[...] + y, g_ref[...], b_ref[...])
    x1_ref[...] = x1
    logits = jnp.dot(x1.astype(BF16), wr_ref[...], preferred_element_type=F32) + br_ref[...]
    lane = lax.broadcasted_iota(jnp.int32, (1, LANES), 1).astype(F32)
    none = float(LANES)
    gl = jnp.where(lane < N_GROUPS, logits, -jnp.inf)
    gmax = jnp.max(gl, axis=1, keepdims=True)
    gidx = jnp.min(jnp.where(gl == gmax, lane, none), axis=1, keepdims=True)
    g_w = 1.0 / jnp.sum(jnp.exp(gl - gmax), axis=1, keepdims=True)
    lo = N_GROUPS + EXPERTS_PER_GROUP * gidx
    el = jnp.where(lane >= lo, jnp.where(lane < lo + EXPERTS_PER_GROUP, logits, -jnp.inf), -jnp.inf)
    e1 = jnp.max(el, axis=1, keepdims=True)
    i1 = jnp.min(jnp.where(el == e1, lane, none), axis=1, keepdims=True)
    el2 = jnp.where(lane == i1, -jnp.inf, el)
    e2 = jnp.max(el2, axis=1, keepdims=True)
    i2 = jnp.min(jnp.where(el2 == e2, lane, none), axis=1, keepdims=True)
    t2 = jnp.exp(e2 - e1)
    w1 = g_w / (1.0 + t2)
    gates_ref[...] = jnp.where(lane == i1, w1, 0.0) + jnp.where(lane == i2, w1 * t2, 0.0)


def _outproj(x2, om, od, wo_b, ln_g, ln_b, wr_b, br, alpha):
    N, D = x2.shape
    tm = min(N, 256)
    w = GROUP_WIDTH
    full = lambda a: pl.BlockSpec(a.shape, lambda t: (0, 0))
    return pl.pallas_call(
        functools.partial(_outproj_kernel, alpha=alpha),
        grid=(N // tm,),
        in_specs=[pl.BlockSpec((tm, D), lambda t: (t, 0)),
                  pl.BlockSpec((tm, w), lambda t: (t, 0)),
                  pl.BlockSpec((tm, w), lambda t: (t, 0)),
                  full(wo_b), full(ln_g), full(ln_b), full(wr_b), full(br)],
        out_specs=[pl.BlockSpec((tm, D), lambda t: (t, 0)), pl.BlockSpec((tm, LANES), lambda t: (t, 0))],
        out_shape=[jax.ShapeDtypeStruct((N, D), F32), jax.ShapeDtypeStruct((N, LANES), F32)],
        compiler_params=_params(("parallel",)),
        name="outproj_router",
    )(x2, om, od, wo_b, ln_g, ln_b, wr_b, br)


def _moe_kernel(x1_ref, gates_ref, wg_ref, wu_ref, wd_ref, g_ref, b_ref, o_ref, xb_sc, acc_sc, *, alpha):
    e = pl.program_id(1)

    @pl.when(e == 0)
    def _():
        xb_sc[...] = x1_ref[...].astype(BF16)
        acc_sc[...] = jnp.zeros_like(acc_sc)

    xb = xb_sc[...]
    a = jnp.dot(xb, wg_ref[0], preferred_element_type=F32)
    u = jnp.dot(xb, wu_ref[0], preferred_element_type=F32)
    lane = lax.broadcasted_iota(jnp.int32, (1, LANES), 1)
    gate = jnp.sum(jnp.where(lane == e + N_GROUPS, gates_ref[...], 0.0), axis=1, keepdims=True)
    hid = (a / (1.0 + jnp.exp(-a))) * u * gate
    acc_sc[...] += jnp.dot(hid.astype(BF16), wd_ref[0], preferred_element_type=F32)

    @pl.when(e == N_EXPERTS - 1)
    def _():
        o_ref[...] = _layer_norm(alpha * x1_ref[...] + acc_sc[...], g_ref[...], b_ref[...])


def _moe(x1, gates, wg_b, wu_b, wd_b, ln_g, ln_b, alpha):
    N, D = x1.shape
    tm = min(N, MOE_TILE)
    f = wg_b.shape[2]
    full = lambda a: pl.BlockSpec(a.shape, lambda t, e: (0, 0))
    return pl.pallas_call(
        functools.partial(_moe_kernel, alpha=alpha),
        grid=(N // tm, N_EXPERTS),
        in_specs=[pl.BlockSpec((tm, D), lambda t, e: (t, 0)),
                  pl.BlockSpec((tm, LANES), lambda t, e: (t, 0)),
                  pl.BlockSpec((1, D, f), lambda t, e: (e, 0, 0)),
                  pl.BlockSpec((1, D, f), lambda t, e: (e, 0, 0)),
                  pl.BlockSpec((1, f, D), lambda t, e: (e, 0, 0)),
                  full(ln_g), full(ln_b)],
        out_specs=pl.BlockSpec((tm, D), lambda t, e: (t, 0)),
        out_shape=jax.ShapeDtypeStruct((N, D), F32),
        scratch_shapes=[pltpu.VMEM((tm, D), BF16), pltpu.VMEM((tm, D), F32)],
        compiler_params=_params(("parallel", "arbitrary")),
        name="moe_ffn",
    )(x1, gates, wg_b, wu_b, wd_b, ln_g, ln_b)


def _moe_decode_kernel(pt_ref, x1_ref, gates_ref, wg_ref, wu_ref, wd_ref, g_ref, b_ref, *refs,
                       alpha, lam_init, ppb, bps, nblk, ups):
    del pt_ref
    npg = ppb * bps
    seq_refs, pages = refs[:9], refs[9:9 + ups * 4 * npg]
    o_ref, om_ref, od_ref, xb_sc, acc_sc = refs[9 + ups * 4 * npg:14 + ups * 4 * npg]
    stats_sc = refs[14 + ups * 4 * npg:]
    e = pl.program_id(1)
    unit0 = (pl.program_id(0) * N_EXPERTS + e) * ups
    steps_per_seq = nblk // bps

    @pl.when(e == 0)
    def _():
        xb_sc[...] = x1_ref[...].astype(BF16)
        acc_sc[...] = jnp.zeros_like(acc_sc)

    xb = xb_sc[...]
    a = jnp.dot(xb, wg_ref[0], preferred_element_type=F32)
    u = jnp.dot(xb, wu_ref[0], preferred_element_type=F32)
    for k in range(ups):
        _decode_partials((unit0 + k) % steps_per_seq, seq_refs[1], seq_refs[2],
                         pages[k * 4 * npg:(k + 1) * 4 * npg], stats_sc, ppb=ppb, bps=bps)
    lane = lax.broadcasted_iota(jnp.int32, (1, LANES), 1)
    gate = jnp.sum(jnp.where(lane == e + N_GROUPS, gates_ref[...], 0.0), axis=1, keepdims=True)
    hid = (a / (1.0 + jnp.exp(-a))) * u * gate
    acc_sc[...] += jnp.dot(hid.astype(BF16), wd_ref[0], preferred_element_type=F32)

    @pl.when(e == N_EXPERTS - 1)
    def _():
        o_ref[...] = _layer_norm(alpha * x1_ref[...] + acc_sc[...], g_ref[...], b_ref[...])

    @pl.when((unit0 + ups - 1) % steps_per_seq == steps_per_seq - 1)
    def _():
        _decode_merge(*seq_refs, om_ref, od_ref, stats_sc, lam_init=lam_init, nblk=nblk)


def _decode_fits_moe(n_tokens, page_table, caches):
    nseq, _, _, nblk, bps = _decode_geometry(page_table, caches)
    moe_steps = (n_tokens // min(n_tokens, MOE_TILE)) * N_EXPERTS
    total, per_seq = nseq * (nblk // bps), nblk // bps
    ups = total // moe_steps
    ok = ups >= 1 and ups * moe_steps == total and per_seq % ups == 0 and ups <= 2
    return ups if ok else 0


def _moe_decode(x1, gates, wg_b, wu_b, wd_b, ln_g, ln_b, alpha,
                page_table, lam_p, qm_b, qd_b, km, vm, kd, vd, caches, g_moba, g_diff, lam_init, ups):
    N, D = x1.shape
    tm = min(N, MOE_TILE)
    f = wg_b.shape[2]
    nseq, npages, ppb, nblk, bps = _decode_geometry(page_table, caches)
    npg = ppb * bps
    per_seq = nblk // bps
    w = GROUP_WIDTH
    full = lambda a: pl.BlockSpec(a.shape, lambda t, e, pt: (0, 0))
    seq_of = lambda t, e: ((t * N_EXPERTS + e) * ups) // per_seq
    row = pl.BlockSpec((1, 1, w), lambda t, e, pt: (seq_of(t, e), 0, 0))
    gain = pl.BlockSpec((1, w), lambda t, e, pt: (0, 0))

    def page_spec(shape, k, a):
        def index(t, e, pt):
            unit = (t * N_EXPERTS + e) * ups + k
            return (pt[(unit // per_seq) * npages + (unit % per_seq) * npg + a], 0, 0)
        return pl.BlockSpec((1,) + shape[1:], index)

    page_specs, page_args = [], []
    for k in range(ups):
        for cache in caches:
            for a in range(npg):
                page_specs.append(page_spec(cache.shape, k, a))
                page_args.append(cache)
    return pl.pallas_call(
        functools.partial(_moe_decode_kernel, alpha=alpha, lam_init=lam_init, ppb=ppb, bps=bps, nblk=nblk, ups=ups),
        grid_spec=pltpu.PrefetchScalarGridSpec(
            num_scalar_prefetch=1,
            grid=(N // tm, N_EXPERTS),
            in_specs=[pl.BlockSpec((tm, D), lambda t, e, pt: (t, 0)),
                      pl.BlockSpec((tm, LANES), lambda t, e, pt: (t, 0)),
                      pl.BlockSpec((1, D, f), lambda t, e, pt: (e, 0, 0)),
                      pl.BlockSpec((1, D, f), lambda t, e, pt: (e, 0, 0)),
                      pl.BlockSpec((1, f, D), lambda t, e, pt: (e, 0, 0)),
                      full(ln_g), full(ln_b),
                      pl.BlockSpec(lam_p.shape, lambda t, e, pt: (0, 0)), row, row, row, row, row, row, gain, gain]
            + page_specs,
            out_specs=[pl.BlockSpec((tm, D), lambda t, e, pt: (t, 0)), row, row],
            scratch_shapes=[pltpu.VMEM((tm, D), BF16), pltpu.VMEM((tm, D), F32)] + _decode_scratch(nblk),
        ),
        out_shape=[jax.ShapeDtypeStruct((N, D), F32)] + [jax.ShapeDtypeStruct((nseq, 1, w), BF16)] * 2,
        compiler_params=_params(("arbitrary", "arbitrary"), VMEM_LIMIT_FUSED),
        name="moe_ffn_decode",
    )(page_table.reshape(-1), x1, gates, wg_b, wu_b, wd_b, ln_g, ln_b,
      lam_p, qm_b, qd_b, km, vm, kd, vd, g_moba.reshape(1, w), g_diff.reshape(1, w), *page_args)


def _mix(x, om, od, lw, alpha):
    B, S, D = x.shape
    w = GROUP_WIDTH
    return _outproj(x.reshape(B * S, D), om.reshape(B * S, w), od.reshape(B * S, w),
                    lw["wo"], lw["ln1_g"], lw["ln1_b"], lw["wr"], lw["br"], alpha)


def kernel(x_prompt, x_sample, cache_moba_k, cache_moba_v, cache_diff_k, cache_diff_v, page_table, w_in, w_out, g_moba, g_diff, lam_q1, lam_k1, lam_q2, lam_k2, ln1_g, ln1_b, w_group, b_group, w_expert, b_expert, w_gate, w_up, w_down, ln2_g, ln2_b):
    depth = w_in.shape[0]
    B, S, D = x_prompt.shape
    nseq, dec_seq, _ = x_sample.shape
    assert dec_seq == 1 and S % MOBA_BLOCK == 0
    n_phys, page = cache_moba_k.shape[1:3]
    past_len = page_table.shape[1] * page
    assert past_len % MOBA_BLOCK == 0 and MOBA_BLOCK % page == 0
    w = GROUP_WIDTH
    alpha = (2 * depth) ** 0.25
    pos_p = jnp.arange(S)
    pos_s = jnp.full((nseq,), past_len, jnp.int32)

    xp, xs = x_prompt, x_sample.reshape(1, nseq, D)
    outs = [[] for _ in range(8)]
    for l in range(depth):
        li = _lambda_init(l)
        w_in_b = w_in[l].astype(BF16)
        wvt_b = jnp.concatenate([w_in_b[:, 2 * w:3 * w], w_in_b[:, 5 * w:6 * w]], axis=1).T
        lam_p = jnp.stack([lam_q1[l], lam_k1[l], lam_q2[l], lam_k2[l]]).astype(F32)
        pad = jnp.zeros((D, LANES - N_GROUPS - N_EXPERTS), F32)
        lw = dict(
            wo=w_out[l].astype(BF16), ln1_g=ln1_g[l].reshape(1, D), ln1_b=ln1_b[l].reshape(1, D),
            wr=jnp.concatenate([w_group[l], w_expert[l], pad], axis=1).astype(BF16),
            br=jnp.concatenate([b_group[l], b_expert[l], pad[0]]).reshape(1, LANES),
            wg=w_gate[l].astype(BF16), wu=w_up[l].astype(BF16), wd=w_down[l].astype(BF16),
            ln2_g=ln2_g[l].reshape(1, D), ln2_b=ln2_b[l].reshape(1, D))
        gm, gd = g_moba[l].reshape(-1), g_diff[l].reshape(-1)

        qm_b, qd_b, km, vm, kd, vd, km_b, kd_b, vmt_b, vdt_b, kmean = _proj(xp, pos_p, w_in_b, wvt_b, prompt=True)
        om, od = _prompt_attention(lam_p, qm_b, km_b, vmt_b, kmean, gm, qd_b, kd_b, vdt_b, gd, li)
        x1, gates = _mix(xp, om, od, lw, alpha)
        ffn = (lw["wg"], lw["wu"], lw["wd"], lw["ln2_g"], lw["ln2_b"], alpha)
        for dst, a in zip(outs[:4], (km, vm, kd, vd)):
            dst.append(a)

        qm2, qd2, km2, vm2, kd2, vd2 = _proj(xs, pos_s, w_in_b, wvt_b, prompt=False)
        caches = [jnp.swapaxes(c[l].reshape(n_phys, page, w), 1, 2)
                  for c in (cache_moba_k, cache_moba_v, cache_diff_k)]
        caches.append(cache_diff_v[l].reshape(n_phys, page * DIFF_HEADS, DIFF_V_DIM))
        per_seq = [a.reshape(nseq, 1, w) for a in (qm2, qd2, km2, vm2, kd2, vd2)]
        ups = _decode_fits_moe(B * S, page_table, caches)
        if ups:
            y, om2, od2 = _moe_decode(x1, gates, *ffn, page_table, lam_p, *per_seq, caches, gm, gd, li, ups)
        else:
            y = _moe(x1, gates, *ffn)
            om2, od2 = _decode_attention(page_table, lam_p, *per_seq, caches, gm, gd, li)
        xp = y.reshape(B, S, D)
        x1s, gates_s = _mix(xs, om2.reshape(1, nseq, w), od2.reshape(1, nseq, w), lw, alpha)
        xs = _moe(x1s, gates_s, *ffn).reshape(1, nseq, D)
        for dst, a in zip(outs[4:], (km2, vm2, kd2, vd2)):
            dst.append(a)

    def stack(parts, tail, lead):
        return jnp.stack(parts).reshape((depth,) + lead + tail)

    mh, dh = (MOBA_HEADS, HEAD_DIM), (DIFF_HEADS, 2, DIFF_QK_DIM)
    dv = (DIFF_HEADS, DIFF_V_DIM)
    return (xp, xs.reshape(nseq, 1, D),
            stack(outs[0], mh, (B, S)), stack(outs[1], mh, (B, S)), stack(outs[2], dh, (B, S)), stack(outs[3], dv, (B, S)),
            stack(outs[4], mh, (nseq, 1)), stack(outs[5], mh, (nseq, 1)), stack(outs[6], dh, (nseq, 1)),
            stack(outs[7], dv, (nseq, 1)))
```

```python
import functools
import math

import jax
import jax.numpy as jnp
from jax import lax
from jax.experimental import pallas as pl
from jax.experimental.pallas import tpu as pltpu

F32 = jnp.float32
BF16 = jnp.bfloat16

LANES = 128
HEAD_DIM = 64
MOBA_HEADS = 8
MOBA_BLOCK = 256
MOBA_TOPK = 3
DIFF_HEADS = 4
DIFF_QK_DIM = 64
DIFF_V_DIM = 128
GROUP_WIDTH = 512
ROT_DIM = HEAD_DIM // 4
ROT_HALF = ROT_DIM // 2
ROPE_THETA = 500000.0
N_GROUPS = 4
EXPERTS_PER_GROUP = 4
N_EXPERTS = N_GROUPS * EXPERTS_PER_GROUP
EXPERT_TOPK = 2
LN_EPS = 1e-5
RMS_EPS = 1e-5
QK_SCALE = HEAD_DIM ** -0.5 * math.log2(math.e)
MASK_BIAS = 2.0 ** 30
ONES_ROWS = 16
KV_UNROLL = 4
ATTN_CHAINS = 4
VMEM_LIMIT = 48 * 1024 * 1024
VMEM_LIMIT_FUSED = 56 * 1024 * 1024
MOE_TILE = 1024
OUTPROJ_TILE = 1024
OUTPROJ_ROWS = 256


def _lambda_init(layer):
    return 0.8 - 0.6 * math.exp(-0.3 * layer)


def _params(sem, vmem=VMEM_LIMIT):
    return pltpu.CompilerParams(dimension_semantics=sem, vmem_limit_bytes=vmem)


def _nt_dot(a, b):
    return lax.dot_general(a, b, (((1,), (1,)), ((), ())), preferred_element_type=F32)


def _layer_norm(z, g, b):
    mu = jnp.mean(z, axis=-1, keepdims=True)
    zc = z - mu
    var = jnp.mean(zc * zc, axis=-1, keepdims=True)
    return zc * lax.rsqrt(var + LN_EPS) * g + b


def _rope_tables(pos):
    n = pos.shape[0]
    inv_freq = ROPE_THETA ** (-jnp.arange(ROT_HALF, dtype=F32) / ROT_HALF)
    ang = pos.astype(F32)[:, None] * inv_freq[None, :]
    cos, sin = jnp.cos(ang), jnp.sin(ang)
    rest = HEAD_DIM - ROT_DIM
    c64 = jnp.concatenate([cos, cos, jnp.ones((n, rest), F32)], axis=1)
    sa64 = jnp.concatenate([-sin, jnp.zeros((n, rest + ROT_HALF), F32)], axis=1)
    sb64 = jnp.concatenate([jnp.zeros((n, ROT_HALF), F32), sin, jnp.zeros((n, rest), F32)], axis=1)
    reps = LANES // HEAD_DIM
    return jnp.tile(c64, (1, reps)), jnp.tile(sa64, (1, reps)), jnp.tile(sb64, (1, reps))


def _rope(h, cos, sa, sb):
    parts = []
    for a in range(h.shape[1] // LANES):
        ha = h[:, a * LANES:(a + 1) * LANES]
        parts.append(ha * cos + pltpu.roll(ha, LANES - ROT_HALF, 1) * sa + pltpu.roll(ha, ROT_HALF, 1) * sb)
    return jnp.concatenate(parts, axis=1)


def _proj_kernel(x_ref, w_ref, cos_ref, sa_ref, sb_ref, *out_refs, prompt):
    if prompt:
        (qm_ref, qd_ref, km_ref, vm_ref, kd_ref, vd_ref,
         kmb_ref, kdb_ref, vmt_ref, vdt_ref, kmean_ref) = out_refs
    else:
        qm_ref, qd_ref, km_ref, vm_ref, kd_ref, vd_ref = out_refs
    xb = x_ref[0].astype(BF16)
    cos, sa, sb = cos_ref[...], sa_ref[...], sb_ref[...]
    w = GROUP_WIDTH

    def col(c):
        return jnp.dot(xb, w_ref[:, c * w:(c + 1) * w], preferred_element_type=F32)

    qm_ref[0] = (_rope(col(0), cos, sa, sb) * QK_SCALE).astype(BF16)
    km = _rope(col(1), cos, sa, sb)
    km_ref[0] = km
    vm = col(2)
    vm_ref[0] = vm
    qd_ref[0] = (_rope(col(3), cos, sa, sb) * QK_SCALE).astype(BF16)
    kd = _rope(col(4), cos, sa, sb)
    kd_ref[0] = kd
    vd = col(5)
    if prompt:
        for h in range(DIFF_HEADS):
            vd_ref[0, pl.ds(h, vd.shape[0], stride=DIFF_HEADS), :] = vd[:, h * DIFF_V_DIM:(h + 1) * DIFF_V_DIM]
        kmb_ref[0] = km.astype(BF16)
        kdb_ref[0] = kd.astype(BF16)
        kmean_ref[0, 0] = jnp.mean(km, axis=0, keepdims=True)
        vmt_ref[0, 0] = vm.T.astype(BF16)
        vdt_ref[0, 0] = vd.T.astype(BF16)
    else:
        vd_ref[0] = vd


def _proj(x, pos, w_in_b, *, prompt):
    B, S, D = x.shape
    tm = min(S, MOBA_BLOCK)
    nt = S // tm
    w = GROUP_WIDTH
    cos, sa, sb = _rope_tables(pos)
    row_spec = pl.BlockSpec((1, tm, w), lambda b, t: (b, t, 0))
    tab_spec = pl.BlockSpec((tm, LANES), lambda b, t: (t, 0))
    out_shape = [jax.ShapeDtypeStruct((B, S, w), BF16)] * 2 + [jax.ShapeDtypeStruct((B, S, w), F32)] * 4
    out_specs = [row_spec] * 6
    if prompt:
        out_shape[5] = jax.ShapeDtypeStruct((B, S * DIFF_HEADS, DIFF_V_DIM), F32)
        out_specs[5] = pl.BlockSpec((1, tm * DIFF_HEADS, DIFF_V_DIM), lambda b, t: (b, t, 0))
        vt_spec = pl.BlockSpec((1, 1, w, tm), lambda b, t: (b, t, 0, 0))
        out_shape += [jax.ShapeDtypeStruct((B, S, w), BF16)] * 2
        out_shape += [jax.ShapeDtypeStruct((B, nt, w, tm), BF16)] * 2
        out_shape += [jax.ShapeDtypeStruct((B, nt, 1, w), F32)]
        out_specs += [row_spec] * 2 + [vt_spec] * 2 + [pl.BlockSpec((1, 1, 1, w), lambda b, t: (b, t, 0, 0))]
    return pl.pallas_call(
        functools.partial(_proj_kernel, prompt=prompt),
        grid=(B, nt),
        in_specs=[pl.BlockSpec((1, tm, D), lambda b, t: (b, t, 0)),
                  pl.BlockSpec(w_in_b.shape, lambda b, t: (0, 0)),
                  tab_spec, tab_spec, tab_spec],
        out_specs=out_specs,
        out_shape=out_shape,
        compiler_params=_params(("parallel", "parallel")),
        name="proj_prompt" if prompt else "proj_sample",
    )(x, w_in_b, cos, sa, sb)


def _group_update(carry, get_score, vts, causal_last=False, next_score=None):
    nblk = len(vts[0])
    pairs = [(c, b) for c in range(len(vts)) for b in range(nblk)]
    maxes, parts = {}, {}
    for k in range(0, len(pairs), 2):
        duo = pairs[k:k + 2]
        if next_score is not None:
            for c, b in duo:
                next_score(c, b)
        probs = {}
        for c, b in duo:
            s = get_score(c, b)
            if causal_last and b == nblk - 1:
                krow = lax.broadcasted_iota(jnp.int32, s.shape, 0)
                qcol = lax.broadcasted_iota(jnp.int32, s.shape, 1)
                s = jnp.where(krow <= qcol, s, -jnp.inf)
            mb = jnp.max(s, axis=0, keepdims=True)
            probs[c, b] = jnp.exp2(s - mb).astype(BF16)
            maxes[c, b] = mb
        for c, b in duo:
            vt = vts[c][b]
            lhs = jnp.concatenate([vt, jnp.ones((ONES_ROWS, vt.shape[1]), BF16)], axis=0)
            parts[c, b] = jnp.dot(lhs, probs[c, b], preferred_element_type=F32)
    out = []
    for c, (m, acc) in enumerate(carry):
        m_new = functools.reduce(jnp.maximum, [maxes[c, b] for b in range(nblk)], m)
        acc = acc * jnp.exp2(m - m_new)
        for b in range(nblk):
            acc = acc + parts[c, b] * jnp.exp2(maxes[c, b] - m_new)
        out.append((m_new, acc))
    return tuple(out)


def _flash(i, nb, tq, vdims, qs, keys_of, values_of, buf):
    buf_a, buf_b = buf.at[0], buf.at[1]
    nchain = len(qs)
    init = tuple((jnp.full((1, tq), -jnp.inf, F32), jnp.zeros((e + ONES_ROWS, tq), F32)) for e in vdims)

    def blocks(g):
        return [jnp.minimum(g * KV_UNROLL + b, nb - 1) for b in range(KV_UNROLL)]

    def scorer(g, dst):
        ks = keys_of(blocks(g))

        def score_tile(c, b):
            dst[c * KV_UNROLL + b] = jnp.dot(ks[c][b], qs[c](), preferred_element_type=F32)
        return score_tile

    def update(carry, src, g, dst):
        return _group_update(carry, lambda c, b: src[c * KV_UNROLL + b], values_of(blocks(g)),
                             next_score=scorer(g + 1, dst))

    def last_update(carry, src, g, nvis):
        return _group_update(carry, lambda c, b: src[c * KV_UNROLL + b], values_of(blocks(g)[:nvis]),
                             causal_last=True)

    nfull = i // KV_UNROLL

    def two_groups(t, carry):
        carry = update(carry, buf_a, 2 * t, buf_b)
        return update(carry, buf_b, 2 * t + 1, buf_a)

    first = scorer(0, buf_a)
    for c in range(nchain):
        for b in range(KV_UNROLL):
            first(c, b)
    carry = lax.fori_loop(0, nfull // 2, two_groups, init)
    odd = nfull % 2
    carry = lax.cond(odd == 1, lambda carry: update(carry, buf_a, nfull - 1, buf_b), lambda carry: carry, carry)
    src = buf.at[odd]
    carry = lax.switch(i - nfull * KV_UNROLL,
                       [functools.partial(last_update, src=src, g=nfull, nvis=n + 1) for n in range(KV_UNROLL)], carry)
    return [acc[:e] / acc[e:e + 1] for (_, acc), e in zip(carry, vdims)]


def _topk_rows(g, k):
    row = lax.broadcasted_iota(jnp.int32, g.shape, 0).astype(F32)
    none = float(g.shape[0])
    sel = jnp.zeros(g.shape, F32)
    for _ in range(k):
        mx = jnp.max(g, axis=0, keepdims=True)
        idx = jnp.min(jnp.where(g == mx, row, none), axis=0, keepdims=True)
        idx = jnp.where(mx > -jnp.inf, idx, none)
        pick = row == idx
        sel = jnp.where(pick, 1.0, sel)
        g = jnp.where(pick, -jnp.inf, g)
    return sel


def _lambda_value(lam_ref, lam_init):
    lp = lam_ref[...]
    a = jnp.sum(lp[0:1] * lp[1:2], axis=1, keepdims=True)
    b = jnp.sum(lp[2:3] * lp[3:4], axis=1, keepdims=True)
    return jnp.exp(a) - jnp.exp(b) + lam_init


def _prompt_attn_kernel(lam_ref, qm_ref, km_ref, vmt_ref, kmean_ref, gm_ref, qd_ref, kd_ref, vdt_ref, gd_ref,
                        om_ref, od_ref, buf, *, lam_init):
    i = pl.program_id(2)
    nb = km_ref.shape[1]
    tq = qm_ref.shape[1]
    lane = lax.broadcasted_iota(jnp.int32, (1, LANES), 1)

    half = lax.broadcasted_iota(jnp.int32, (LANES, tq), 0) >> 6
    qmt = qm_ref[0].astype(F32).T
    kmean = kmean_ref[0].astype(BF16)
    brow = lax.broadcasted_iota(jnp.int32, (nb, tq), 0)
    qhts = [jnp.where(half == h, qmt, 0.0).astype(BF16) for h in range(LANES // HEAD_DIM)]
    gates = [jnp.dot(kmean, qht, preferred_element_type=F32) for qht in qhts]

    def moba_query(h):
        qht = qhts[h]
        sel = _topk_rows(jnp.where(brow < i, gates[h], -jnp.inf), MOBA_TOPK)
        bias_t = jnp.where(brow == i, 0.0, jnp.where(sel > 0.0, 0.0, -MASK_BIAS))
        if nb < LANES:
            bias_t = jnp.concatenate([bias_t, jnp.zeros((LANES - nb, tq), F32)], axis=0)
        return jnp.concatenate([qht, bias_t.astype(BF16)], axis=0)

    qdt = qd_ref[0].astype(F32).T
    qs = [functools.cache(functools.partial(lambda r: jnp.where(half == r, qdt, 0.0).astype(BF16), r))
          for r in range(2)]
    qs += [functools.cache(functools.partial(moba_query, h)) for h in range(LANES // HEAD_DIM)]

    def moba_keys(j):
        onehot = jnp.where(lane == j, 1.0, 0.0).astype(BF16)
        return jnp.concatenate([km_ref[0, j], jnp.broadcast_to(onehot, (MOBA_BLOCK, LANES))], axis=1)

    def keys_of(js):
        mk = [moba_keys(j) for j in js]
        dk = [kd_ref[0, j] for j in js]
        return [dk, dk, mk, mk]

    def values_of(js):
        dv = [vdt_ref[0, j] for j in js]
        return [dv, dv] + [[vmt_ref[0, j, h * HEAD_DIM:(h + 1) * HEAD_DIM, :] for j in js] for h in range(2)]

    d0, d1, o0, o1 = _flash(i, nb, tq, (DIFF_V_DIM, DIFF_V_DIM, HEAD_DIM, HEAD_DIM), qs, keys_of, values_of,
                            buf)

    outs = []
    for o in (o0, o1):
        ms = jnp.mean(o * o, axis=0, keepdims=True)
        outs.append(o * lax.rsqrt(ms + RMS_EPS))
    om_ref[0] = (jnp.concatenate(outs, axis=0).T * gm_ref[...]).astype(BF16)

    o = d0 - _lambda_value(lam_ref, lam_init) * d1
    ms = jnp.mean(o * o, axis=0, keepdims=True)
    o = o * lax.rsqrt(ms + RMS_EPS)
    od_ref[0] = ((o.T * gd_ref[...]) * (1.0 - lam_init)).astype(BF16)


def _prompt_attention(lam_p, qm_b, km_b, vmt_b, kmean, g_moba, qd_b, kd_b, vdt_b, g_diff, lam_init):
    B, S, w = qm_b.shape
    nb = S // MOBA_BLOCK
    assert nb % KV_UNROLL == 0 and w // LANES == DIFF_HEADS
    tq = MOBA_BLOCK
    q_spec = pl.BlockSpec((1, tq, LANES), lambda b, p, i: (b, i, p))
    k_spec = pl.BlockSpec((1, nb, tq, LANES), lambda b, p, i: (b, 0, 0, p))
    vt_spec = pl.BlockSpec((1, nb, LANES, tq), lambda b, p, i: (b, 0, p, 0))
    gain = pl.BlockSpec((1, LANES), lambda b, p, i: (0, p))
    return pl.pallas_call(
        functools.partial(_prompt_attn_kernel, lam_init=lam_init),
        grid=(B, DIFF_HEADS, nb),
        in_specs=[pl.BlockSpec(lam_p.shape, lambda b, p, i: (0, 0)),
                  q_spec, k_spec, vt_spec, pl.BlockSpec((1, nb, LANES), lambda b, p, i: (b, 0, p)), gain,
                  q_spec, k_spec, vt_spec, gain],
        out_specs=[q_spec, q_spec],
        out_shape=[jax.ShapeDtypeStruct((B, S, w), BF16)] * 2,
        scratch_shapes=[pltpu.VMEM((2, ATTN_CHAINS * KV_UNROLL, tq, tq), F32)],
        compiler_params=_params(("parallel", "parallel", "arbitrary")),
        name="prompt_attention",
    )(lam_p, qm_b, km_b.reshape(B, nb, tq, w), vmt_b, kmean.reshape(B, nb, w), g_moba.reshape(1, w),
      qd_b, kd_b.reshape(B, nb, tq, w), vdt_b, g_diff.reshape(1, w))


SROWS = 16
DECODE_BLOCKS_PER_STEP = 2


def _block_scores(q_ref, kt_refs):
    r = lax.broadcasted_iota(jnp.int32, (SROWS, GROUP_WIDTH), 0)
    c = lax.broadcasted_iota(jnp.int32, (SROWS, GROUP_WIDTH), 1)
    qrow = jnp.broadcast_to(q_ref[0].astype(F32), (SROWS, GROUP_WIDTH))
    qmat = jnp.where((c >> 6) == r, qrow, 0.0).astype(BF16)
    return jnp.concatenate([jnp.dot(qmat, kr[0].astype(BF16), preferred_element_type=F32) for kr in kt_refs], axis=1)


def _block_values(pb, v_refs, v_by_head):
    page = pb.shape[1] // len(v_refs)
    acc = None
    for a, vr in enumerate(v_refs):
        pa = pb[:, a * page:(a + 1) * page]
        if v_by_head:
            part = jnp.concatenate(
                [jnp.dot(pa, vr[0, pl.ds(h, page, stride=DIFF_HEADS), :].astype(BF16), preferred_element_type=F32)
                 for h in range(DIFF_HEADS)], axis=1)
        else:
            part = _nt_dot(pa, vr[0].astype(BF16))
        acc = part if acc is None else acc + part
    return acc


def _topk_list(gs, k):
    none = float(len(gs))
    gs = list(gs)
    sel = [jnp.zeros(g.shape, F32) for g in gs]
    for _ in range(k):
        mx = functools.reduce(jnp.maximum, gs)
        idx = functools.reduce(jnp.minimum, [jnp.where(g == mx, float(b), none) for b, g in enumerate(gs)])
        idx = jnp.where(mx > -jnp.inf, idx, none)
        for b in range(len(gs)):
            pick = idx == float(b)
            sel[b] = jnp.where(pick, 1.0, sel[b])
            gs[b] = jnp.where(pick, -jnp.inf, gs[b])
    return sel


def _merge_partials(sel, m_sc, l_sc, r_sc, s_new, v_new, nblk):
    m = s_new
    for j in range(nblk):
        mj = m_sc[j][:, 0:1]
        m = jnp.maximum(m, mj if sel is None else jnp.where(sel[j] > 0.0, mj, -jnp.inf))
    w_new = jnp.exp2(s_new - m)
    l = w_new
    racc = w_new * v_new
    for j in range(nblk):
        wj = jnp.exp2(m_sc[j][:, 0:1] - m)
        if sel is not None:
            wj = jnp.where(sel[j] > 0.0, wj, 0.0)
        l = l + wj * l_sc[j][:, 0:1]
        racc = racc + wj * r_sc[j]
    return racc / l


def _decode_partials(step, qm_ref, qd_ref, pages, stats_sc, *, ppb, bps):
    npg = ppb * bps
    gm_sc, mm_sc, lm_sc, rm_sc, md_sc, ld_sc, rd_sc = stats_sc
    mk, mv, dk, dv = (pages[a * npg:(a + 1) * npg] for a in range(4))
    units = [(u, diff) for u in range(bps) for diff in (False, True)]
    scores = {(u, diff): _block_scores(qd_ref if diff else qm_ref, (dk if diff else mk)[u * ppb:(u + 1) * ppb])
              for u, diff in units}
    stats = {}
    for key in units:
        s = scores[key]
        m = jnp.max(s, axis=1, keepdims=True)
        p = jnp.exp2(s - m)
        stats[key] = (jnp.sum(s, axis=1, keepdims=True), m, jnp.sum(p, axis=1, keepdims=True), p.astype(BF16))
    for u, diff in units:
        g, m, l, pb = stats[u, diff]
        r = _block_values(pb, (dv if diff else mv)[u * ppb:(u + 1) * ppb], diff)
        j = step * bps + u
        m_sc, l_sc, r_sc = (md_sc, ld_sc, rd_sc) if diff else (mm_sc, lm_sc, rm_sc)
        m_sc[j] = jnp.broadcast_to(m, (SROWS, LANES))
        l_sc[j] = jnp.broadcast_to(l, (SROWS, LANES))
        r_sc[j] = r
        if not diff:
            gm_sc[j] = jnp.broadcast_to(g, (SROWS, LANES))


def _decode_merge(lam_ref, qm_ref, qd_ref, kmn_ref, vmn_ref, kdn_ref, vdn_ref, gm_ref, gd_ref, om_ref, od_ref,
                  stats_sc, *, lam_init, nblk):
    gm_sc, mm_sc, lm_sc, rm_sc, md_sc, ld_sc, rd_sc = stats_sc
    r16 = lax.broadcasted_iota(jnp.int32, (SROWS, GROUP_WIDTH), 0)
    c16 = lax.broadcasted_iota(jnp.int32, (SROWS, GROUP_WIDTH), 1)

    def new_score(q_ref, k_ref):
        qf = jnp.where((c16 >> 6) == r16, jnp.broadcast_to(q_ref[0].astype(F32), (SROWS, GROUP_WIDTH)), 0.0)
        return jnp.sum(qf * k_ref[0], axis=1, keepdims=True)

    sel = _topk_list([gm_sc[b][:, 0:1] for b in range(nblk)], MOBA_TOPK)
    o = _merge_partials(sel, mm_sc, lm_sc, rm_sc, new_score(qm_ref, kmn_ref), vmn_ref[0], nblk)
    own = (c16 >> 6) == r16
    ms = jnp.sum(jnp.where(own, o * o, 0.0), axis=1, keepdims=True) * (1.0 / HEAD_DIM)
    o = jnp.where(own, o * lax.rsqrt(ms + RMS_EPS), 0.0)
    om_ref[0] = (jnp.sum(o, axis=0, keepdims=True) * gm_ref[...]).astype(BF16)

    lam = _lambda_value(lam_ref, lam_init)
    o = _merge_partials(None, md_sc, ld_sc, rd_sc, new_score(qd_ref, kdn_ref), vdn_ref[0], nblk)
    signed = jnp.where((r16 & 1) == 0, o, -lam * o)
    o = jnp.sum(jnp.where((c16 >> 7) == (r16 >> 1), signed, 0.0), axis=0, keepdims=True)
    segs = []
    for h in range(DIFF_HEADS):
        seg = o[:, h * DIFF_V_DIM:(h + 1) * DIFF_V_DIM]
        ms = jnp.mean(seg * seg, axis=1, keepdims=True)
        segs.append(seg * lax.rsqrt(ms + RMS_EPS))
    od_ref[0] = ((jnp.concatenate(segs, axis=1) * gd_ref[...]) * (1.0 - lam_init)).astype(BF16)


def _decode_kernel(pt_ref, *refs, lam_init, ppb, bps, nblk):
    del pt_ref
    npg = ppb * bps
    seq_refs, pages = refs[:9], refs[9:9 + 4 * npg]
    om_ref, od_ref = refs[9 + 4 * npg:11 + 4 * npg]
    stats_sc = refs[11 + 4 * npg:]
    step = pl.program_id(1)
    _decode_partials(step, seq_refs[1], seq_refs[2], pages, stats_sc, ppb=ppb, bps=bps)

    @pl.when(step == nblk // bps - 1)
    def _():
        _decode_merge(*seq_refs, om_ref, od_ref, stats_sc, lam_init=lam_init, nblk=nblk)


def _decode_geometry(page_table, caches):
    nseq, npages = page_table.shape
    page = caches[0].shape[2]
    ppb = MOBA_BLOCK // page
    nblk = npages // ppb
    bps = DECODE_BLOCKS_PER_STEP if nblk % DECODE_BLOCKS_PER_STEP == 0 else 1
    return nseq, npages, ppb, nblk, bps


def _decode_scratch(nblk):
    stat = pltpu.VMEM((nblk, SROWS, LANES), F32)
    part = pltpu.VMEM((nblk, SROWS, GROUP_WIDTH), F32)
    return [stat, stat, stat, part, stat, stat, part]


def _decode_attention(page_table, lam_p, qm_b, qd_b, km, vm, kd, vd, caches, g_moba, g_diff, lam_init):
    nseq, npages, ppb, nblk, bps = _decode_geometry(page_table, caches)
    npg = ppb * bps
    w = GROUP_WIDTH
    row = pl.BlockSpec((1, 1, w), lambda b, j, pt: (b, 0, 0))
    gain = pl.BlockSpec((1, w), lambda b, j, pt: (0, 0))

    def page_spec(shape, a):
        return pl.BlockSpec((1,) + shape[1:], lambda b, j, pt, a=a: (pt[b * npages + j * npg + a], 0, 0))

    page_specs, page_args = [], []
    for cache in caches:
        for a in range(npg):
            page_specs.append(page_spec(cache.shape, a))
            page_args.append(cache)
    return pl.pallas_call(
        functools.partial(_decode_kernel, lam_init=lam_init, ppb=ppb, bps=bps, nblk=nblk),
        grid_spec=pltpu.PrefetchScalarGridSpec(
            num_scalar_prefetch=1,
            grid=(nseq, nblk // bps),
            in_specs=[pl.BlockSpec(lam_p.shape, lambda b, j, pt: (0, 0)), row, row, row, row, row, row, gain, gain]
            + page_specs,
            out_specs=[row, row],
            scratch_shapes=_decode_scratch(nblk),
        ),
        out_shape=[jax.ShapeDtypeStruct((nseq, 1, w), BF16)] * 2,
        compiler_params=_params(("parallel", "arbitrary")),
        name="decode_attention",
    )(page_table.reshape(-1), lam_p, qm_b, qd_b, km, vm, kd, vd, g_moba.reshape(1, w), g_diff.reshape(1, w),
      *page_args)


def _route(logits):
    lane = lax.broadcasted_iota(jnp.int32, (1, LANES), 1).astype(F32)
    none = float(LANES)
    gl = jnp.where(lane < N_GROUPS, logits, -jnp.inf)
    gmax = jnp.max(gl, axis=1, keepdims=True)
    gidx = jnp.min(jnp.where(gl == gmax, lane, none), axis=1, keepdims=True)
    g_w = 1.0 / jnp.sum(jnp.exp(gl - gmax), axis=1, keepdims=True)
    lo = N_GROUPS + EXPERTS_PER_GROUP * gidx
    el = jnp.where(lane >= lo, jnp.where(lane < lo + EXPERTS_PER_GROUP, logits, -jnp.inf), -jnp.inf)
    e1 = jnp.max(el, axis=1, keepdims=True)
    i1 = jnp.min(jnp.where(el == e1, lane, none), axis=1, keepdims=True)
    el2 = jnp.where(lane == i1, -jnp.inf, el)
    e2 = jnp.max(el2, axis=1, keepdims=True)
    i2 = jnp.min(jnp.where(el2 == e2, lane, none), axis=1, keepdims=True)
    t2 = jnp.exp(e2 - e1)
    w1 = g_w / (1.0 + t2)
    return jnp.where(lane == i1, w1, 0.0) + jnp.where(lane == i2, w1 * t2, 0.0)


def _outproj_kernel(x_ref, om_ref, od_ref, wo_ref, g_ref, b_ref, wr_ref, br_ref, x1_ref, gates_ref, *, alpha):
    w = GROUP_WIDTH
    nsub = max(1, x_ref.shape[0] // OUTPROJ_ROWS)
    subs = [pl.ds(s * (x_ref.shape[0] // nsub), x_ref.shape[0] // nsub) for s in range(nsub)]
    ys = [jnp.dot(om_ref[r, :], wo_ref[0:w, :], preferred_element_type=F32)
          + jnp.dot(od_ref[r, :], wo_ref[w:2 * w, :], preferred_element_type=F32) for r in subs]
    for r, y in zip(subs, ys):
        x1 = _layer_norm(alpha * x_ref[r, :] + y, g_ref[...], b_ref[...])
        x1_ref[r, :] = x1
        gates_ref[r, :] = _route(jnp.dot(x1.astype(BF16), wr_ref[...], preferred_element_type=F32) + br_ref[...])


def _outproj(x2, om, od, wo_b, ln_g, ln_b, wr_b, br, alpha):
    N, D = x2.shape
    tm = min(N, OUTPROJ_TILE)
    w = GROUP_WIDTH
    full = lambda a: pl.BlockSpec(a.shape, lambda t: (0, 0))
    return pl.pallas_call(
        functools.partial(_outproj_kernel, alpha=alpha),
        grid=(N // tm,),
        in_specs=[pl.BlockSpec((tm, D), lambda t: (t, 0)),
                  pl.BlockSpec((tm, w), lambda t: (t, 0)),
                  pl.BlockSpec((tm, w), lambda t: (t, 0)),
                  full(wo_b), full(ln_g), full(ln_b), full(wr_b), full(br)],
        out_specs=[pl.BlockSpec((tm, D), lambda t: (t, 0)), pl.BlockSpec((tm, LANES), lambda t: (t, 0))],
        out_shape=[jax.ShapeDtypeStruct((N, D), F32), jax.ShapeDtypeStruct((N, LANES), F32)],
        compiler_params=_params(("parallel",)),
        name="outproj_router",
    )(x2, om, od, wo_b, ln_g, ln_b, wr_b, br)


def _moe_kernel(x1_ref, gates_ref, wg_ref, wu_ref, wd_ref, g_ref, b_ref, o_ref, xb_sc, acc_sc, *, alpha):
    e = pl.program_id(1)

    @pl.when(e == 0)
    def _():
        xb_sc[...] = x1_ref[...].astype(BF16)
        acc_sc[...] = jnp.zeros_like(acc_sc)

    xb = xb_sc[...]
    a = jnp.dot(xb, wg_ref[0], preferred_element_type=F32)
    u = jnp.dot(xb, wu_ref[0], preferred_element_type=F32)
    lane = lax.broadcasted_iota(jnp.int32, (1, LANES), 1)
    gate = jnp.sum(jnp.where(lane == e + N_GROUPS, gates_ref[...], 0.0), axis=1, keepdims=True)
    hid = (a / (1.0 + jnp.exp(-a))) * u * gate
    acc_sc[...] += jnp.dot(hid.astype(BF16), wd_ref[0], preferred_element_type=F32)

    @pl.when(e == N_EXPERTS - 1)
    def _():
        o_ref[...] = _layer_norm(alpha * x1_ref[...] + acc_sc[...], g_ref[...], b_ref[...])


def _moe(x1, gates, wg_b, wu_b, wd_b, ln_g, ln_b, alpha):
    N, D = x1.shape
    tm = min(N, MOE_TILE)
    f = wg_b.shape[2]
    full = lambda a: pl.BlockSpec(a.shape, lambda t, e: (0, 0))
    return pl.pallas_call(
        functools.partial(_moe_kernel, alpha=alpha),
        grid=(N // tm, N_EXPERTS),
        in_specs=[pl.BlockSpec((tm, D), lambda t, e: (t, 0)),
                  pl.BlockSpec((tm, LANES), lambda t, e: (t, 0)),
                  pl.BlockSpec((1, D, f), lambda t, e: (e, 0, 0)),
                  pl.BlockSpec((1, D, f), lambda t, e: (e, 0, 0)),
                  pl.BlockSpec((1, f, D), lambda t, e: (e, 0, 0)),
                  full(ln_g), full(ln_b)],
        out_specs=pl.BlockSpec((tm, D), lambda t, e: (t, 0)),
        out_shape=jax.ShapeDtypeStruct((N, D), F32),
        scratch_shapes=[pltpu.VMEM((tm, D), BF16), pltpu.VMEM((tm, D), F32)],
        compiler_params=_params(("parallel", "arbitrary")),
        name="moe_ffn",
    )(x1, gates, wg_b, wu_b, wd_b, ln_g, ln_b)


def _moe_decode_kernel(pt_ref, x1_ref, gates_ref, wg_ref, wu_ref, wd_ref, g_ref, b_ref, *refs,
                       alpha, lam_init, ppb, bps, nblk, ups):
    del pt_ref
    npg = ppb * bps
    seq_refs, pages = refs[:9], refs[9:9 + ups * 4 * npg]
    o_ref, om_ref, od_ref, xb_sc, acc_sc = refs[9 + ups * 4 * npg:14 + ups * 4 * npg]
    stats_sc = refs[14 + ups * 4 * npg:]
    e = pl.program_id(1)
    unit0 = (pl.program_id(0) * N_EXPERTS + e) * ups
    steps_per_seq = nblk // bps

    @pl.when(e == 0)
    def _():
        xb_sc[...] = x1_ref[...].astype(BF16)
        acc_sc[...] = jnp.zeros_like(acc_sc)

    xb = xb_sc[...]
    a = jnp.dot(xb, wg_ref[0], preferred_element_type=F32)
    u = jnp.dot(xb, wu_ref[0], preferred_element_type=F32)
    for k in range(ups):
        _decode_partials((unit0 + k) % steps_per_seq, seq_refs[1], seq_refs[2],
                         pages[k * 4 * npg:(k + 1) * 4 * npg], stats_sc, ppb=ppb, bps=bps)
    lane = lax.broadcasted_iota(jnp.int32, (1, LANES), 1)
    gate = jnp.sum(jnp.where(lane == e + N_GROUPS, gates_ref[...], 0.0), axis=1, keepdims=True)
    hid = (a / (1.0 + jnp.exp(-a))) * u * gate
    acc_sc[...] += jnp.dot(hid.astype(BF16), wd_ref[0], preferred_element_type=F32)

    @pl.when(e == N_EXPERTS - 1)
    def _():
        o_ref[...] = _layer_norm(alpha * x1_ref[...] + acc_sc[...], g_ref[...], b_ref[...])

    @pl.when((unit0 + ups - 1) % steps_per_seq == steps_per_seq - 1)
    def _():
        _decode_merge(*seq_refs, om_ref, od_ref, stats_sc, lam_init=lam_init, nblk=nblk)


def _decode_fits_moe(n_tokens, page_table, caches):
    nseq, _, _, nblk, bps = _decode_geometry(page_table, caches)
    moe_steps = (n_tokens // min(n_tokens, MOE_TILE)) * N_EXPERTS
    total, per_seq = nseq * (nblk // bps), nblk // bps
    ups = total // moe_steps
    ok = ups >= 1 and ups * moe_steps == total and per_seq % ups == 0 and ups <= 2
    return ups if ok else 0


def _moe_decode(x1, gates, wg_b, wu_b, wd_b, ln_g, ln_b, alpha,
                page_table, lam_p, qm_b, qd_b, km, vm, kd, vd, caches, g_moba, g_diff, lam_init, ups):
    N, D = x1.shape
    tm = min(N, MOE_TILE)
    f = wg_b.shape[2]
    nseq, npages, ppb, nblk, bps = _decode_geometry(page_table, caches)
    npg = ppb * bps
    per_seq = nblk // bps
    w = GROUP_WIDTH
    full = lambda a: pl.BlockSpec(a.shape, lambda t, e, pt: (0, 0))
    seq_of = lambda t, e: ((t * N_EXPERTS + e) * ups) // per_seq
    row = pl.BlockSpec((1, 1, w), lambda t, e, pt: (seq_of(t, e), 0, 0))
    gain = pl.BlockSpec((1, w), lambda t, e, pt: (0, 0))

    def page_spec(shape, k, a):
        def index(t, e, pt):
            unit = (t * N_EXPERTS + e) * ups + k
            return (pt[(unit // per_seq) * npages + (unit % per_seq) * npg + a], 0, 0)
        return pl.BlockSpec((1,) + shape[1:], index)

    page_specs, page_args = [], []
    for k in range(ups):
        for cache in caches:
            for a in range(npg):
                page_specs.append(page_spec(cache.shape, k, a))
                page_args.append(cache)
    return pl.pallas_call(
        functools.partial(_moe_decode_kernel, alpha=alpha, lam_init=lam_init, ppb=ppb, bps=bps, nblk=nblk, ups=ups),
        grid_spec=pltpu.PrefetchScalarGridSpec(
            num_scalar_prefetch=1,
            grid=(N // tm, N_EXPERTS),
            in_specs=[pl.BlockSpec((tm, D), lambda t, e, pt: (t, 0)),
                      pl.BlockSpec((tm, LANES), lambda t, e, pt: (t, 0)),
                      pl.BlockSpec((1, D, f), lambda t, e, pt: (e, 0, 0)),
                      pl.BlockSpec((1, D, f), lambda t, e, pt: (e, 0, 0)),
                      pl.BlockSpec((1, f, D), lambda t, e, pt: (e, 0, 0)),
                      full(ln_g), full(ln_b),
                      pl.BlockSpec(lam_p.shape, lambda t, e, pt: (0, 0)), row, row, row, row, row, row, gain, gain]
            + page_specs,
            out_specs=[pl.BlockSpec((tm, D), lambda t, e, pt: (t, 0)), row, row],
            scratch_shapes=[pltpu.VMEM((tm, D), BF16), pltpu.VMEM((tm, D), F32)] + _decode_scratch(nblk),
        ),
        out_shape=[jax.ShapeDtypeStruct((N, D), F32)] + [jax.ShapeDtypeStruct((nseq, 1, w), BF16)] * 2,
        compiler_params=_params(("arbitrary", "arbitrary"), VMEM_LIMIT_FUSED),
        name="moe_ffn_decode",
    )(page_table.reshape(-1), x1, gates, wg_b, wu_b, wd_b, ln_g, ln_b,
      lam_p, qm_b, qd_b, km, vm, kd, vd, g_moba.reshape(1, w), g_diff.reshape(1, w), *page_args)


def _mix(x, om, od, lw, alpha):
    B, S, D = x.shape
    w = GROUP_WIDTH
    return _outproj(x.reshape(B * S, D), om.reshape(B * S, w), od.reshape(B * S, w),
                    lw["wo"], lw["ln1_g"], lw["ln1_b"], lw["wr"], lw["br"], alpha)


def kernel(x_prompt, x_sample, cache_moba_k, cache_moba_v, cache_diff_k, cache_diff_v, page_table, w_in, w_out, g_moba, g_diff, lam_q1, lam_k1, lam_q2, lam_k2, ln1_g, ln1_b, w_group, b_group, w_expert, b_expert, w_gate, w_up, w_down, ln2_g, ln2_b):
    depth = w_in.shape[0]
    B, S, D = x_prompt.shape
    nseq, dec_seq, _ = x_sample.shape
    assert dec_seq == 1 and S % MOBA_BLOCK == 0
    n_phys, page = cache_moba_k.shape[1:3]
    past_len = page_table.shape[1] * page
    assert past_len % MOBA_BLOCK == 0 and MOBA_BLOCK % page == 0
    w = GROUP_WIDTH
    alpha = (2 * depth) ** 0.25
    pos_p = jnp.arange(S)
    pos_s = jnp.full((nseq,), past_len, jnp.int32)

    xp, xs = x_prompt, x_sample.reshape(1, nseq, D)
    outs = [[] for _ in range(8)]
    for l in range(depth):
        li = _lambda_init(l)
        w_in_b = w_in[l].astype(BF16)
        lam_p = jnp.stack([lam_q1[l], lam_k1[l], lam_q2[l], lam_k2[l]]).astype(F32)
        pad = jnp.zeros((D, LANES - N_GROUPS - N_EXPERTS), F32)
        lw = dict(
            wo=w_out[l].astype(BF16), ln1_g=ln1_g[l].reshape(1, D), ln1_b=ln1_b[l].reshape(1, D),
            wr=jnp.concatenate([w_group[l], w_expert[l], pad], axis=1).astype(BF16),
            br=jnp.concatenate([b_group[l], b_expert[l], pad[0]]).reshape(1, LANES),
            wg=w_gate[l].astype(BF16), wu=w_up[l].astype(BF16), wd=w_down[l].astype(BF16),
            ln2_g=ln2_g[l].reshape(1, D), ln2_b=ln2_b[l].reshape(1, D))
        gm, gd = g_moba[l].reshape(-1), g_diff[l].reshape(-1)

        qm_b, qd_b, km, vm, kd, vd, km_b, kd_b, vmt_b, vdt_b, kmean = _proj(xp, pos_p, w_in_b, prompt=True)
        om, od = _prompt_attention(lam_p, qm_b, km_b, vmt_b, kmean, gm, qd_b, kd_b, vdt_b, gd, li)
        x1, gates = _mix(xp, om, od, lw, alpha)
        ffn = (lw["wg"], lw["wu"], lw["wd"], lw["ln2_g"], lw["ln2_b"], alpha)
        for dst, a in zip(outs[:4], (km, vm, kd, vd)):
            dst.append(a)

        qm2, qd2, km2, vm2, kd2, vd2 = _proj(xs, pos_s, w_in_b, prompt=False)
        caches = [jnp.swapaxes(c[l].reshape(n_phys, page, w), 1, 2)
                  for c in (cache_moba_k, cache_moba_v, cache_diff_k)]
        caches.append(cache_diff_v[l].reshape(n_phys, page * DIFF_HEADS, DIFF_V_DIM))
        per_seq = [a.reshape(nseq, 1, w) for a in (qm2, qd2, km2, vm2, kd2, vd2)]
        ups = _decode_fits_moe(B * S, page_table, caches)
        if ups:
            y, om2, od2 = _moe_decode(x1, gates, *ffn, page_table, lam_p, *per_seq, caches, gm, gd, li, ups)
        else:
            y = _moe(x1, gates, *ffn)
            om2, od2 = _decode_attention(page_table, lam_p, *per_seq, caches, gm, gd, li)
        xp = y.reshape(B, S, D)
        x1s, gates_s = _mix(xs, om2.reshape(1, nseq, w), od2.reshape(1, nseq, w), lw, alpha)
        xs = _moe(x1s, gates_s, *ffn).reshape(1, nseq, D)
        for dst, a in zip(outs[4:], (km2, vm2, kd2, vd2)):
            dst.append(a)

    def stack(parts, tail, lead):
        return jnp.stack(parts).reshape((depth,) + lead + tail)

    mh, dh = (MOBA_HEADS, HEAD_DIM), (DIFF_HEADS, 2, DIFF_QK_DIM)
    dv = (DIFF_HEADS, DIFF_V_DIM)
    return (xp, xs.reshape(nseq, 1, D),
            stack(outs[0], mh, (B, S)), stack(outs[1], mh, (B, S)), stack(outs[2], dh, (B, S)), stack(outs[3], dv, (B, S)),
            stack(outs[4], mh, (nseq, 1)), stack(outs[5], mh, (nseq, 1)), stack(outs[6], dh, (nseq, 1)),
            stack(outs[7], dv, (nseq, 1)))
```

```python
import functools
import math

import jax
import jax.numpy as jnp
from jax import lax
from jax.experimental import pallas as pl
from jax.experimental.pallas import tpu as pltpu

F32 = jnp.float32
BF16 = jnp.bfloat16

LANES = 128
HEAD_DIM = 64
MOBA_HEADS = 8
MOBA_BLOCK = 256
MOBA_TOPK = 3
DIFF_HEADS = 4
DIFF_QK_DIM = 64
DIFF_V_DIM = 128
GROUP_WIDTH = 512
ROT_DIM = HEAD_DIM // 4
ROT_HALF = ROT_DIM // 2
ROPE_THETA = 500000.0
N_GROUPS = 4
EXPERTS_PER_GROUP = 4
N_EXPERTS = N_GROUPS * EXPERTS_PER_GROUP
EXPERT_TOPK = 2
LN_EPS = 1e-5
RMS_EPS = 1e-5
QK_SCALE = HEAD_DIM ** -0.5 * math.log2(math.e)
MASK_BIAS = 2.0 ** 30
ONES_ROWS = 16
KV_UNROLL = 4
ATTN_CHAINS = 4
VMEM_LIMIT = 48 * 1024 * 1024
VMEM_LIMIT_FUSED = 56 * 1024 * 1024
MOE_TILE = 1024
OUTPROJ_TILE = 1024
OUTPROJ_ROWS = 256


def _lambda_init(layer):
    return 0.8 - 0.6 * math.exp(-0.3 * layer)


def _params(sem, vmem=VMEM_LIMIT):
    return pltpu.CompilerParams(dimension_semantics=sem, vmem_limit_bytes=vmem)


def _nt_dot(a, b):
    return lax.dot_general(a, b, (((1,), (1,)), ((), ())), preferred_element_type=F32)


def _layer_norm(z, g, b):
    mu = jnp.mean(z, axis=-1, keepdims=True)
    zc = z - mu
    var = jnp.mean(zc * zc, axis=-1, keepdims=True)
    return zc * lax.rsqrt(var + LN_EPS) * g + b


def _rope_tables(pos):
    n = pos.shape[0]
    inv_freq = ROPE_THETA ** (-jnp.arange(ROT_HALF, dtype=F32) / ROT_HALF)
    ang = pos.astype(F32)[:, None] * inv_freq[None, :]
    cos, sin = jnp.cos(ang), jnp.sin(ang)
    rest = HEAD_DIM - ROT_DIM
    c64 = jnp.concatenate([cos, cos, jnp.ones((n, rest), F32)], axis=1)
    sa64 = jnp.concatenate([-sin, jnp.zeros((n, rest + ROT_HALF), F32)], axis=1)
    sb64 = jnp.concatenate([jnp.zeros((n, ROT_HALF), F32), sin, jnp.zeros((n, rest), F32)], axis=1)
    reps = LANES // HEAD_DIM
    return jnp.tile(c64, (1, reps)), jnp.tile(sa64, (1, reps)), jnp.tile(sb64, (1, reps))


def _rope(h, cos, sa, sb):
    parts = []
    for a in range(h.shape[1] // LANES):
        ha = h[:, a * LANES:(a + 1) * LANES]
        parts.append(ha * cos + pltpu.roll(ha, LANES - ROT_HALF, 1) * sa + pltpu.roll(ha, ROT_HALF, 1) * sb)
    return jnp.concatenate(parts, axis=1)


def _proj_kernel(x_ref, w_ref, cos_ref, sa_ref, sb_ref, *out_refs, prompt):
    if prompt:
        (qm_ref, qd_ref, km_ref, vm_ref, kd_ref, vd_ref,
         kmb_ref, kdb_ref, vmt_ref, vdt_ref, kmean_ref) = out_refs
    else:
        qm_ref, qd_ref, km_ref, vm_ref, kd_ref, vd_ref = out_refs
    xb = x_ref[0].astype(BF16)
    cos, sa, sb = cos_ref[...], sa_ref[...], sb_ref[...]
    w = GROUP_WIDTH

    def col(c):
        return jnp.dot(xb, w_ref[:, c * w:(c + 1) * w], preferred_element_type=F32)

    qm_ref[0] = (_rope(col(0), cos, sa, sb) * QK_SCALE).astype(BF16)
    km = _rope(col(1), cos, sa, sb)
    km_ref[0] = km
    vm = col(2)
    vm_ref[0] = vm
    qd_ref[0] = (_rope(col(3), cos, sa, sb) * QK_SCALE).astype(BF16)
    kd = _rope(col(4), cos, sa, sb)
    kd_ref[0] = kd
    vd = col(5)
    if prompt:
        for h in range(DIFF_HEADS):
            vd_ref[0, pl.ds(h, vd.shape[0], stride=DIFF_HEADS), :] = vd[:, h * DIFF_V_DIM:(h + 1) * DIFF_V_DIM]
        kmb_ref[0] = km.astype(BF16)
        kdb_ref[0] = kd.astype(BF16)
        kmean_ref[0, 0] = jnp.mean(km, axis=0, keepdims=True)
        vmt_ref[0, 0] = vm.T.astype(BF16)
        vdt_ref[0, 0] = vd.T.astype(BF16)
    else:
        vd_ref[0] = vd


def _proj(x, pos, w_in_b, *, prompt):
    B, S, D = x.shape
    tm = min(S, MOBA_BLOCK)
    nt = S // tm
    w = GROUP_WIDTH
    cos, sa, sb = _rope_tables(pos)
    row_spec = pl.BlockSpec((1, tm, w), lambda b, t: (b, t, 0))
    tab_spec = pl.BlockSpec((tm, LANES), lambda b, t: (t, 0))
    out_shape = [jax.ShapeDtypeStruct((B, S, w), BF16)] * 2 + [jax.ShapeDtypeStruct((B, S, w), F32)] * 4
    out_specs = [row_spec] * 6
    if prompt:
        out_shape[5] = jax.ShapeDtypeStruct((B, S * DIFF_HEADS, DIFF_V_DIM), F32)
        out_specs[5] = pl.BlockSpec((1, tm * DIFF_HEADS, DIFF_V_DIM), lambda b, t: (b, t, 0))
        vt_spec = pl.BlockSpec((1, 1, w, tm), lambda b, t: (b, t, 0, 0))
        out_shape += [jax.ShapeDtypeStruct((B, S, w), BF16)] * 2
        out_shape += [jax.ShapeDtypeStruct((B, nt, w, tm), BF16)] * 2
        out_shape += [jax.ShapeDtypeStruct((B, nt, 1, w), F32)]
        out_specs += [row_spec] * 2 + [vt_spec] * 2 + [pl.BlockSpec((1, 1, 1, w), lambda b, t: (b, t, 0, 0))]
    return pl.pallas_call(
        functools.partial(_proj_kernel, prompt=prompt),
        grid=(B, nt),
        in_specs=[pl.BlockSpec((1, tm, D), lambda b, t: (b, t, 0)),
                  pl.BlockSpec(w_in_b.shape, lambda b, t: (0, 0)),
                  tab_spec, tab_spec, tab_spec],
        out_specs=out_specs,
        out_shape=out_shape,
        compiler_params=_params(("parallel", "parallel")),
        name="proj_prompt" if prompt else "proj_sample",
    )(x, w_in_b, cos, sa, sb)


def _group_update(carry, get_score, vts, causal_last=False, next_score=None):
    nblk = len(vts[0])
    pairs = [(c, b) for c in range(len(vts)) for b in range(nblk)]
    maxes, parts = {}, {}
    for k in range(0, len(pairs), 2):
        duo = pairs[k:k + 2]
        if next_score is not None:
            for c, b in duo:
                next_score(c, b)
        probs = {}
        for c, b in duo:
            s = get_score(c, b)
            if causal_last and b == nblk - 1:
                krow = lax.broadcasted_iota(jnp.int32, s.shape, 0)
                qcol = lax.broadcasted_iota(jnp.int32, s.shape, 1)
                s = jnp.where(krow <= qcol, s, -jnp.inf)
            mb = jnp.max(s, axis=0, keepdims=True)
            probs[c, b] = jnp.exp2(s - mb).astype(BF16)
            maxes[c, b] = mb
        for c, b in duo:
            vt = vts[c][b]
            lhs = jnp.concatenate([vt, jnp.ones((ONES_ROWS, vt.shape[1]), BF16)], axis=0)
            parts[c, b] = jnp.dot(lhs, probs[c, b], preferred_element_type=F32)
    out = []
    for c, (m, acc) in enumerate(carry):
        m_new = functools.reduce(jnp.maximum, [maxes[c, b] for b in range(nblk)], m)
        acc = acc * jnp.exp2(m - m_new)
        for b in range(nblk):
            acc = acc + parts[c, b] * jnp.exp2(maxes[c, b] - m_new)
        out.append((m_new, acc))
    return tuple(out)


def _flash(i, nb, tq, vdims, qs, keys_of, values_of, buf):
    buf_a, buf_b = buf.at[0], buf.at[1]
    nchain = len(qs)
    init = tuple((jnp.full((1, tq), -jnp.inf, F32), jnp.zeros((e + ONES_ROWS, tq), F32)) for e in vdims)

    def blocks(g):
        return [jnp.minimum(g * KV_UNROLL + b, nb - 1) for b in range(KV_UNROLL)]

    def scorer(g, dst):
        ks = keys_of(blocks(g))

        def score_tile(c, b):
            dst[c * KV_UNROLL + b] = jnp.dot(ks[c][b], qs[c](), preferred_element_type=F32)
        return score_tile

    def update(carry, src, g, dst):
        return _group_update(carry, lambda c, b: src[c * KV_UNROLL + b], values_of(blocks(g)),
                             next_score=scorer(g + 1, dst))

    def last_update(carry, src, g, nvis):
        return _group_update(carry, lambda c, b: src[c * KV_UNROLL + b], values_of(blocks(g)[:nvis]),
                             causal_last=True)

    nfull = i // KV_UNROLL

    def two_groups(t, carry):
        carry = update(carry, buf_a, 2 * t, buf_b)
        return update(carry, buf_b, 2 * t + 1, buf_a)

    first = scorer(0, buf_a)
    for c in range(nchain):
        for b in range(KV_UNROLL):
            first(c, b)
    carry = lax.fori_loop(0, nfull // 2, two_groups, init)
    odd = nfull % 2
    carry = lax.cond(odd == 1, lambda carry: update(carry, buf_a, nfull - 1, buf_b), lambda carry: carry, carry)
    src = buf.at[odd]
    carry = lax.switch(i - nfull * KV_UNROLL,
                       [functools.partial(last_update, src=src, g=nfull, nvis=n + 1) for n in range(KV_UNROLL)], carry)
    return [acc[:e] / acc[e:e + 1] for (_, acc), e in zip(carry, vdims)]


def _topk_rows(g, k):
    row = lax.broadcasted_iota(jnp.int32, g.shape, 0).astype(F32)
    none = float(g.shape[0])
    sel = jnp.zeros(g.shape, F32)
    for _ in range(k):
        mx = jnp.max(g, axis=0, keepdims=True)
        idx = jnp.min(jnp.where(g == mx, row, none), axis=0, keepdims=True)
        idx = jnp.where(mx > -jnp.inf, idx, none)
        pick = row == idx
        sel = jnp.where(pick, 1.0, sel)
        g = jnp.where(pick, -jnp.inf, g)
    return sel


def _lambda_value(lam_ref, lam_init):
    lp = lam_ref[...]
    a = jnp.sum(lp[0:1] * lp[1:2], axis=1, keepdims=True)
    b = jnp.sum(lp[2:3] * lp[3:4], axis=1, keepdims=True)
    return jnp.exp(a) - jnp.exp(b) + lam_init


def _prompt_attn_kernel(lam_ref, qm_ref, km_ref, vmt_ref, kmean_ref, gm_ref, qd_ref, kd_ref, vdt_ref, gd_ref,
                        om_ref, od_ref, buf, *, lam_init):
    i = pl.program_id(2)
    nb = km_ref.shape[1]
    tq = qm_ref.shape[1]
    lane = lax.broadcasted_iota(jnp.int32, (1, LANES), 1)

    half = lax.broadcasted_iota(jnp.int32, (LANES, tq), 0) >> 6
    qmt = qm_ref[0].astype(F32).T
    kmean = kmean_ref[0].astype(BF16)
    brow = lax.broadcasted_iota(jnp.int32, (nb, tq), 0)
    qhts = [jnp.where(half == h, qmt, 0.0).astype(BF16) for h in range(LANES // HEAD_DIM)]
    gates = [jnp.dot(kmean, qht, preferred_element_type=F32) for qht in qhts]

    def moba_query(h):
        qht = qhts[h]
        sel = _topk_rows(jnp.where(brow < i, gates[h], -jnp.inf), MOBA_TOPK)
        bias_t = jnp.where(brow == i, 0.0, jnp.where(sel > 0.0, 0.0, -MASK_BIAS))
        if nb < LANES:
            bias_t = jnp.concatenate([bias_t, jnp.zeros((LANES - nb, tq), F32)], axis=0)
        return jnp.concatenate([qht, bias_t.astype(BF16)], axis=0)

    qdt = qd_ref[0].astype(F32).T
    qs = [functools.cache(functools.partial(lambda r: jnp.where(half == r, qdt, 0.0).astype(BF16), r))
          for r in range(2)]
    qs += [functools.cache(functools.partial(moba_query, h)) for h in range(LANES // HEAD_DIM)]

    def moba_keys(j):
        onehot = jnp.where(lane == j, 1.0, 0.0).astype(BF16)
        return jnp.concatenate([km_ref[0, j], jnp.broadcast_to(onehot, (MOBA_BLOCK, LANES))], axis=1)

    def keys_of(js):
        mk = [moba_keys(j) for j in js]
        dk = [kd_ref[0, j] for j in js]
        return [dk, dk, mk, mk]

    def values_of(js):
        dv = [vdt_ref[0, j] for j in js]
        return [dv, dv] + [[vmt_ref[0, j, h * HEAD_DIM:(h + 1) * HEAD_DIM, :] for j in js] for h in range(2)]

    d0, d1, o0, o1 = _flash(i, nb, tq, (DIFF_V_DIM, DIFF_V_DIM, HEAD_DIM, HEAD_DIM), qs, keys_of, values_of,
                            buf)

    outs = []
    for o in (o0, o1):
        ms = jnp.mean(o * o, axis=0, keepdims=True)
        outs.append(o * lax.rsqrt(ms + RMS_EPS))
    om_ref[0] = (jnp.concatenate(outs, axis=0).T * gm_ref[...]).astype(BF16)

    o = d0 - _lambda_value(lam_ref, lam_init) * d1
    ms = jnp.mean(o * o, axis=0, keepdims=True)
    o = o * lax.rsqrt(ms + RMS_EPS)
    od_ref[0] = ((o.T * gd_ref[...]) * (1.0 - lam_init)).astype(BF16)


def _prompt_attention(lam_p, qm_b, km_b, vmt_b, kmean, g_moba, qd_b, kd_b, vdt_b, g_diff, lam_init):
    B, S, w = qm_b.shape
    nb = S // MOBA_BLOCK
    assert nb % KV_UNROLL == 0 and w // LANES == DIFF_HEADS
    tq = MOBA_BLOCK
    q_spec = pl.BlockSpec((1, tq, LANES), lambda b, p, i: (b, i, p))
    k_spec = pl.BlockSpec((1, nb, tq, LANES), lambda b, p, i: (b, 0, 0, p))
    vt_spec = pl.BlockSpec((1, nb, LANES, tq), lambda b, p, i: (b, 0, p, 0))
    gain = pl.BlockSpec((1, LANES), lambda b, p, i: (0, p))
    return pl.pallas_call(
        functools.partial(_prompt_attn_kernel, lam_init=lam_init),
        grid=(B, DIFF_HEADS, nb),
        in_specs=[pl.BlockSpec(lam_p.shape, lambda b, p, i: (0, 0)),
                  q_spec, k_spec, vt_spec, pl.BlockSpec((1, nb, LANES), lambda b, p, i: (b, 0, p)), gain,
                  q_spec, k_spec, vt_spec, gain],
        out_specs=[q_spec, q_spec],
        out_shape=[jax.ShapeDtypeStruct((B, S, w), BF16)] * 2,
        scratch_shapes=[pltpu.VMEM((2, ATTN_CHAINS * KV_UNROLL, tq, tq), F32)],
        compiler_params=_params(("parallel", "parallel", "arbitrary")),
        name="prompt_attention",
    )(lam_p, qm_b, km_b.reshape(B, nb, tq, w), vmt_b, kmean.reshape(B, nb, w), g_moba.reshape(1, w),
      qd_b, kd_b.reshape(B, nb, tq, w), vdt_b, g_diff.reshape(1, w))


SROWS = 16
DECODE_BLOCKS_PER_STEP = 2


def _block_scores(q_ref, kt_refs):
    r = lax.broadcasted_iota(jnp.int32, (SROWS, GROUP_WIDTH), 0)
    c = lax.broadcasted_iota(jnp.int32, (SROWS, GROUP_WIDTH), 1)
    qrow = jnp.broadcast_to(q_ref[0].astype(F32), (SROWS, GROUP_WIDTH))
    qmat = jnp.where((c >> 6) == r, qrow, 0.0).astype(BF16)
    kt = jnp.concatenate([kr[0].astype(BF16) for kr in kt_refs], axis=1)
    return jnp.dot(qmat, kt, preferred_element_type=F32)


def _block_values(pb, v_refs, v_by_head):
    page = pb.shape[1] // len(v_refs)
    if v_by_head:
        v = jnp.concatenate(
            [jnp.concatenate([vr[0, pl.ds(h, page, stride=DIFF_HEADS), :].astype(BF16) for h in range(DIFF_HEADS)],
                             axis=1) for vr in v_refs], axis=0)
        return jnp.dot(pb, v, preferred_element_type=F32)
    vt = jnp.concatenate([vr[0].astype(BF16) for vr in v_refs], axis=1)
    return _nt_dot(pb, vt)


def _topk_list(gs, k):
    none = float(len(gs))
    gs = list(gs)
    sel = [jnp.zeros(g.shape, F32) for g in gs]
    for _ in range(k):
        mx = functools.reduce(jnp.maximum, gs)
        idx = functools.reduce(jnp.minimum, [jnp.where(g == mx, float(b), none) for b, g in enumerate(gs)])
        idx = jnp.where(mx > -jnp.inf, idx, none)
        for b in range(len(gs)):
            pick = idx == float(b)
            sel[b] = jnp.where(pick, 1.0, sel[b])
            gs[b] = jnp.where(pick, -jnp.inf, gs[b])
    return sel


def _merge_partials(sel, m_sc, l_sc, r_sc, s_new, v_new, nblk):
    m = s_new
    for j in range(nblk):
        mj = m_sc[j][:, 0:1]
        m = jnp.maximum(m, mj if sel is None else jnp.where(sel[j] > 0.0, mj, -jnp.inf))
    w_new = jnp.exp2(s_new - m)
    l = w_new
    racc = w_new * v_new
    for j in range(nblk):
        wj = jnp.exp2(m_sc[j][:, 0:1] - m)
        if sel is not None:
            wj = jnp.where(sel[j] > 0.0, wj, 0.0)
        l = l + wj * l_sc[j][:, 0:1]
        racc = racc + wj * r_sc[j]
    return racc / l


def _decode_partials(step, qm_ref, qd_ref, pages, stats_sc, *, ppb, bps):
    npg = ppb * bps
    gm_sc, mm_sc, lm_sc, rm_sc, md_sc, ld_sc, rd_sc = stats_sc
    mk, mv, dk, dv = (pages[a * npg:(a + 1) * npg] for a in range(4))
    units = [(u, diff) for u in range(bps) for diff in (False, True)]
    scores = {(u, diff): _block_scores(qd_ref if diff else qm_ref, (dk if diff else mk)[u * ppb:(u + 1) * ppb])
              for u, diff in units}
    stats = {}
    for key in units:
        s = scores[key]
        m = jnp.max(s, axis=1, keepdims=True)
        p = jnp.exp2(s - m)
        stats[key] = (jnp.sum(s, axis=1, keepdims=True), m, jnp.sum(p, axis=1, keepdims=True), p.astype(BF16))
    for u, diff in units:
        g, m, l, pb = stats[u, diff]
        r = _block_values(pb, (dv if diff else mv)[u * ppb:(u + 1) * ppb], diff)
        j = step * bps + u
        m_sc, l_sc, r_sc = (md_sc, ld_sc, rd_sc) if diff else (mm_sc, lm_sc, rm_sc)
        m_sc[j] = jnp.broadcast_to(m, (SROWS, LANES))
        l_sc[j] = jnp.broadcast_to(l, (SROWS, LANES))
        r_sc[j] = r
        if not diff:
            gm_sc[j] = jnp.broadcast_to(g, (SROWS, LANES))


def _decode_merge(lam_ref, qm_ref, qd_ref, kmn_ref, vmn_ref, kdn_ref, vdn_ref, gm_ref, gd_ref, om_ref, od_ref,
                  stats_sc, *, lam_init, nblk):
    gm_sc, mm_sc, lm_sc, rm_sc, md_sc, ld_sc, rd_sc = stats_sc
    r16 = lax.broadcasted_iota(jnp.int32, (SROWS, GROUP_WIDTH), 0)
    c16 = lax.broadcasted_iota(jnp.int32, (SROWS, GROUP_WIDTH), 1)

    def new_score(q_ref, k_ref):
        qf = jnp.where((c16 >> 6) == r16, jnp.broadcast_to(q_ref[0].astype(F32), (SROWS, GROUP_WIDTH)), 0.0)
        return jnp.sum(qf * k_ref[0], axis=1, keepdims=True)

    sel = _topk_list([gm_sc[b][:, 0:1] for b in range(nblk)], MOBA_TOPK)
    o = _merge_partials(sel, mm_sc, lm_sc, rm_sc, new_score(qm_ref, kmn_ref), vmn_ref[0], nblk)
    own = (c16 >> 6) == r16
    ms = jnp.sum(jnp.where(own, o * o, 0.0), axis=1, keepdims=True) * (1.0 / HEAD_DIM)
    o = jnp.where(own, o * lax.rsqrt(ms + RMS_EPS), 0.0)
    om_ref[0] = (jnp.sum(o, axis=0, keepdims=True) * gm_ref[...]).astype(BF16)

    lam = _lambda_value(lam_ref, lam_init)
    o = _merge_partials(None, md_sc, ld_sc, rd_sc, new_score(qd_ref, kdn_ref), vdn_ref[0], nblk)
    signed = jnp.where((r16 & 1) == 0, o, -lam * o)
    o = jnp.sum(jnp.where((c16 >> 7) == (r16 >> 1), signed, 0.0), axis=0, keepdims=True)
    segs = []
    for h in range(DIFF_HEADS):
        seg = o[:, h * DIFF_V_DIM:(h + 1) * DIFF_V_DIM]
        ms = jnp.mean(seg * seg, axis=1, keepdims=True)
        segs.append(seg * lax.rsqrt(ms + RMS_EPS))
    od_ref[0] = ((jnp.concatenate(segs, axis=1) * gd_ref[...]) * (1.0 - lam_init)).astype(BF16)


def _decode_kernel(pt_ref, *refs, lam_init, ppb, bps, nblk):
    del pt_ref
    npg = ppb * bps
    seq_refs, pages = refs[:9], refs[9:9 + 4 * npg]
    om_ref, od_ref = refs[9 + 4 * npg:11 + 4 * npg]
    stats_sc = refs[11 + 4 * npg:]
    step = pl.program_id(1)
    _decode_partials(step, seq_refs[1], seq_refs[2], pages, stats_sc, ppb=ppb, bps=bps)

    @pl.when(step == nblk // bps - 1)
    def _():
        _decode_merge(*seq_refs, om_ref, od_ref, stats_sc, lam_init=lam_init, nblk=nblk)


def _decode_geometry(page_table, caches):
    nseq, npages = page_table.shape
    page = caches[0].shape[2]
    ppb = MOBA_BLOCK // page
    nblk = npages // ppb
    bps = DECODE_BLOCKS_PER_STEP if nblk % DECODE_BLOCKS_PER_STEP == 0 else 1
    return nseq, npages, ppb, nblk, bps


def _decode_scratch(nblk):
    stat = pltpu.VMEM((nblk, SROWS, LANES), F32)
    part = pltpu.VMEM((nblk, SROWS, GROUP_WIDTH), F32)
    return [stat, stat, stat, part, stat, stat, part]


def _decode_attention(page_table, lam_p, qm_b, qd_b, km, vm, kd, vd, caches, g_moba, g_diff, lam_init):
    nseq, npages, ppb, nblk, bps = _decode_geometry(page_table, caches)
    npg = ppb * bps
    w = GROUP_WIDTH
    row = pl.BlockSpec((1, 1, w), lambda b, j, pt: (b, 0, 0))
    gain = pl.BlockSpec((1, w), lambda b, j, pt: (0, 0))

    def page_spec(shape, a):
        return pl.BlockSpec((1,) + shape[1:], lambda b, j, pt, a=a: (pt[b * npages + j * npg + a], 0, 0))

    page_specs, page_args = [], []
    for cache in caches:
        for a in range(npg):
            page_specs.append(page_spec(cache.shape, a))
            page_args.append(cache)
    return pl.pallas_call(
        functools.partial(_decode_kernel, lam_init=lam_init, ppb=ppb, bps=bps, nblk=nblk),
        grid_spec=pltpu.PrefetchScalarGridSpec(
            num_scalar_prefetch=1,
            grid=(nseq, nblk // bps),
            in_specs=[pl.BlockSpec(lam_p.shape, lambda b, j, pt: (0, 0)), row, row, row, row, row, row, gain, gain]
            + page_specs,
            out_specs=[row, row],
            scratch_shapes=_decode_scratch(nblk),
        ),
        out_shape=[jax.ShapeDtypeStruct((nseq, 1, w), BF16)] * 2,
        compiler_params=_params(("parallel", "arbitrary")),
        name="decode_attention",
    )(page_table.reshape(-1), lam_p, qm_b, qd_b, km, vm, kd, vd, g_moba.reshape(1, w), g_diff.reshape(1, w),
      *page_args)


def _route(logits):
    lane = lax.broadcasted_iota(jnp.int32, (1, LANES), 1).astype(F32)
    none = float(LANES)
    gl = jnp.where(lane < N_GROUPS, logits, -jnp.inf)
    gmax = jnp.max(gl, axis=1, keepdims=True)
    gidx = jnp.min(jnp.where(gl == gmax, lane, none), axis=1, keepdims=True)
    g_w = 1.0 / jnp.sum(jnp.exp(gl - gmax), axis=1, keepdims=True)
    lo = N_GROUPS + EXPERTS_PER_GROUP * gidx
    el = jnp.where(lane >= lo, jnp.where(lane < lo + EXPERTS_PER_GROUP, logits, -jnp.inf), -jnp.inf)
    e1 = jnp.max(el, axis=1, keepdims=True)
    i1 = jnp.min(jnp.where(el == e1, lane, none), axis=1, keepdims=True)
    el2 = jnp.where(lane == i1, -jnp.inf, el)
    e2 = jnp.max(el2, axis=1, keepdims=True)
    i2 = jnp.min(jnp.where(el2 == e2, lane, none), axis=1, keepdims=True)
    t2 = jnp.exp(e2 - e1)
    w1 = g_w / (1.0 + t2)
    return jnp.where(lane == i1, w1, 0.0) + jnp.where(lane == i2, w1 * t2, 0.0)


def _outproj_kernel(x_ref, om_ref, od_ref, wo_ref, g_ref, b_ref, wr_ref, br_ref, x1_ref, gates_ref, *, alpha):
    w = GROUP_WIDTH
    nsub = max(1, x_ref.shape[0] // OUTPROJ_ROWS)
    subs = [pl.ds(s * (x_ref.shape[0] // nsub), x_ref.shape[0] // nsub) for s in range(nsub)]
    ys = [jnp.dot(om_ref[r, :], wo_ref[0:w, :], preferred_element_type=F32)
          + jnp.dot(od_ref[r, :], wo_ref[w:2 * w, :], preferred_element_type=F32) for r in subs]
    for r, y in zip(subs, ys):
        x1 = _layer_norm(alpha * x_ref[r, :] + y, g_ref[...], b_ref[...])
        x1_ref[r, :] = x1
        gates_ref[r, :] = _route(jnp.dot(x1.astype(BF16), wr_ref[...], preferred_element_type=F32) + br_ref[...])


def _outproj(x2, om, od, wo_b, ln_g, ln_b, wr_b, br, alpha):
    N, D = x2.shape
    tm = min(N, OUTPROJ_TILE)
    w = GROUP_WIDTH
    full = lambda a: pl.BlockSpec(a.shape, lambda t: (0, 0))
    return pl.pallas_call(
        functools.partial(_outproj_kernel, alpha=alpha),
        grid=(N // tm,),
        in_specs=[pl.BlockSpec((tm, D), lambda t: (t, 0)),
                  pl.BlockSpec((tm, w), lambda t: (t, 0)),
                  pl.BlockSpec((tm, w), lambda t: (t, 0)),
                  full(wo_b), full(ln_g), full(ln_b), full(wr_b), full(br)],
        out_specs=[pl.BlockSpec((tm, D), lambda t: (t, 0)), pl.BlockSpec((tm, LANES), lambda t: (t, 0))],
        out_shape=[jax.ShapeDtypeStruct((N, D), F32), jax.ShapeDtypeStruct((N, LANES), F32)],
        compiler_params=_params(("parallel",)),
        name="outproj_router",
    )(x2, om, od, wo_b, ln_g, ln_b, wr_b, br)


def _moe_kernel(x1_ref, gates_ref, wg_ref, wu_ref, wd_ref, g_ref, b_ref, o_ref, xb_sc, acc_sc, *, alpha):
    e = pl.program_id(1)

    @pl.when(e == 0)
    def _():
        xb_sc[...] = x1_ref[...].astype(BF16)
        acc_sc[...] = jnp.zeros_like(acc_sc)

    xb = xb_sc[...]
    a = jnp.dot(xb, wg_ref[0], preferred_element_type=F32)
    u = jnp.dot(xb, wu_ref[0], preferred_element_type=F32)
    lane = lax.broadcasted_iota(jnp.int32, (1, LANES), 1)
    gate = jnp.sum(jnp.where(lane == e + N_GROUPS, gates_ref[...], 0.0), axis=1, keepdims=True)
    hid = (a / (1.0 + jnp.exp(-a))) * u * gate
    acc_sc[...] += jnp.dot(hid.astype(BF16), wd_ref[0], preferred_element_type=F32)

    @pl.when(e == N_EXPERTS - 1)
    def _():
        o_ref[...] = _layer_norm(alpha * x1_ref[...] + acc_sc[...], g_ref[...], b_ref[...])


def _moe(x1, gates, wg_b, wu_b, wd_b, ln_g, ln_b, alpha):
    N, D = x1.shape
    tm = min(N, MOE_TILE)
    f = wg_b.shape[2]
    full = lambda a: pl.BlockSpec(a.shape, lambda t, e: (0, 0))
    return pl.pallas_call(
        functools.partial(_moe_kernel, alpha=alpha),
        grid=(N // tm, N_EXPERTS),
        in_specs=[pl.BlockSpec((tm, D), lambda t, e: (t, 0)),
                  pl.BlockSpec((tm, LANES), lambda t, e: (t, 0)),
                  pl.BlockSpec((1, D, f), lambda t, e: (e, 0, 0)),
                  pl.BlockSpec((1, D, f), lambda t, e: (e, 0, 0)),
                  pl.BlockSpec((1, f, D), lambda t, e: (e, 0, 0)),
                  full(ln_g), full(ln_b)],
        out_specs=pl.BlockSpec((tm, D), lambda t, e: (t, 0)),
        out_shape=jax.ShapeDtypeStruct((N, D), F32),
        scratch_shapes=[pltpu.VMEM((tm, D), BF16), pltpu.VMEM((tm, D), F32)],
        compiler_params=_params(("parallel", "arbitrary")),
        name="moe_ffn",
    )(x1, gates, wg_b, wu_b, wd_b, ln_g, ln_b)


def _moe_decode_kernel(pt_ref, x1_ref, gates_ref, wg_ref, wu_ref, wd_ref, g_ref, b_ref, *refs,
                       alpha, lam_init, ppb, bps, nblk, ups):
    del pt_ref
    npg = ppb * bps
    seq_refs, pages = refs[:9], refs[9:9 + ups * 4 * npg]
    o_ref, om_ref, od_ref, xb_sc, acc_sc = refs[9 + ups * 4 * npg:14 + ups * 4 * npg]
    stats_sc = refs[14 + ups * 4 * npg:]
    e = pl.program_id(1)
    unit0 = (pl.program_id(0) * N_EXPERTS + e) * ups
    steps_per_seq = nblk // bps

    @pl.when(e == 0)
    def _():
        xb_sc[...] = x1_ref[...].astype(BF16)
        acc_sc[...] = jnp.zeros_like(acc_sc)

    xb = xb_sc[...]
    a = jnp.dot(xb, wg_ref[0], preferred_element_type=F32)
    u = jnp.dot(xb, wu_ref[0], preferred_element_type=F32)
    joined = [pages[(k * 4 + cache) * npg + a] for cache in range(4) for k in range(ups) for a in range(npg)]
    _decode_partials((unit0 % steps_per_seq) // ups, seq_refs[1], seq_refs[2], joined, stats_sc,
                     ppb=ppb, bps=bps * ups)
    lane = lax.broadcasted_iota(jnp.int32, (1, LANES), 1)
    gate = jnp.sum(jnp.where(lane == e + N_GROUPS, gates_ref[...], 0.0), axis=1, keepdims=True)
    hid = (a / (1.0 + jnp.exp(-a))) * u * gate
    acc_sc[...] += jnp.dot(hid.astype(BF16), wd_ref[0], preferred_element_type=F32)

    @pl.when(e == N_EXPERTS - 1)
    def _():
        o_ref[...] = _layer_norm(alpha * x1_ref[...] + acc_sc[...], g_ref[...], b_ref[...])

    @pl.when((unit0 + ups - 1) % steps_per_seq == steps_per_seq - 1)
    def _():
        _decode_merge(*seq_refs, om_ref, od_ref, stats_sc, lam_init=lam_init, nblk=nblk)


def _decode_fits_moe(n_tokens, page_table, caches):
    nseq, _, _, nblk, bps = _decode_geometry(page_table, caches)
    moe_steps = (n_tokens // min(n_tokens, MOE_TILE)) * N_EXPERTS
    total, per_seq = nseq * (nblk // bps), nblk // bps
    ups = total // moe_steps
    ok = ups >= 1 and ups * moe_steps == total and per_seq % ups == 0 and ups <= 2
    return ups if ok else 0


def _moe_decode(x1, gates, wg_b, wu_b, wd_b, ln_g, ln_b, alpha,
                page_table, lam_p, qm_b, qd_b, km, vm, kd, vd, caches, g_moba, g_diff, lam_init, ups):
    N, D = x1.shape
    tm = min(N, MOE_TILE)
    f = wg_b.shape[2]
    nseq, npages, ppb, nblk, bps = _decode_geometry(page_table, caches)
    npg = ppb * bps
    per_seq = nblk // bps
    w = GROUP_WIDTH
    full = lambda a: pl.BlockSpec(a.shape, lambda t, e, pt: (0, 0))
    seq_of = lambda t, e: ((t * N_EXPERTS + e) * ups) // per_seq
    row = pl.BlockSpec((1, 1, w), lambda t, e, pt: (seq_of(t, e), 0, 0))
    gain = pl.BlockSpec((1, w), lambda t, e, pt: (0, 0))

    def page_spec(shape, k, a):
        def index(t, e, pt):
            return (pt[((t * N_EXPERTS + e) * ups + k) * npg + a], 0, 0)
        return pl.BlockSpec((1,) + shape[1:], index)

    page_specs, page_args = [], []
    for k in range(ups):
        for cache in caches:
            for a in range(npg):
                page_specs.append(page_spec(cache.shape, k, a))
                page_args.append(cache)
    return pl.pallas_call(
        functools.partial(_moe_decode_kernel, alpha=alpha, lam_init=lam_init, ppb=ppb, bps=bps, nblk=nblk, ups=ups),
        grid_spec=pltpu.PrefetchScalarGridSpec(
            num_scalar_prefetch=1,
            grid=(N // tm, N_EXPERTS),
            in_specs=[pl.BlockSpec((tm, D), lambda t, e, pt: (t, 0)),
                      pl.BlockSpec((tm, LANES), lambda t, e, pt: (t, 0)),
                      pl.BlockSpec((1, D, f), lambda t, e, pt: (e, 0, 0)),
                      pl.BlockSpec((1, D, f), lambda t, e, pt: (e, 0, 0)),
                      pl.BlockSpec((1, f, D), lambda t, e, pt: (e, 0, 0)),
                      full(ln_g), full(ln_b),
                      pl.BlockSpec(lam_p.shape, lambda t, e, pt: (0, 0)), row, row, row, row, row, row, gain, gain]
            + page_specs,
            out_specs=[pl.BlockSpec((tm, D), lambda t, e, pt: (t, 0)), row, row],
            scratch_shapes=[pltpu.VMEM((tm, D), BF16), pltpu.VMEM((tm, D), F32)] + _decode_scratch(nblk),
        ),
        out_shape=[jax.ShapeDtypeStruct((N, D), F32)] + [jax.ShapeDtypeStruct((nseq, 1, w), BF16)] * 2,
        compiler_params=_params(("arbitrary", "arbitrary"), VMEM_LIMIT_FUSED),
        name="moe_ffn_decode",
    )(page_table.reshape(-1), x1, gates, wg_b, wu_b, wd_b, ln_g, ln_b,
      lam_p, qm_b, qd_b, km, vm, kd, vd, g_moba.reshape(1, w), g_diff.reshape(1, w), *page_args)


def _mix(x, om, od, lw, alpha):
    B, S, D = x.shape
    w = GROUP_WIDTH
    return _outproj(x.reshape(B * S, D), om.reshape(B * S, w), od.reshape(B * S, w),
                    lw["wo"], lw["ln1_g"], lw["ln1_b"], lw["wr"], lw["br"], alpha)


def kernel(x_prompt, x_sample, cache_moba_k, cache_moba_v, cache_diff_k, cache_diff_v, page_table, w_in, w_out, g_moba, g_diff, lam_q1, lam_k1, lam_q2, lam_k2, ln1_g, ln1_b, w_group, b_group, w_expert, b_expert, w_gate, w_up, w_down, ln2_g, ln2_b):
    depth = w_in.shape[0]
    B, S, D = x_prompt.shape
    nseq, dec_seq, _ = x_sample.shape
    assert dec_seq == 1 and S % MOBA_BLOCK == 0
    n_phys, page = cache_moba_k.shape[1:3]
    past_len = page_table.shape[1] * page
    assert past_len % MOBA_BLOCK == 0 and MOBA_BLOCK % page == 0
    w = GROUP_WIDTH
    alpha = (2 * depth) ** 0.25
    pos_p = jnp.arange(S)
    pos_s = jnp.full((nseq,), past_len, jnp.int32)

    xp, xs = x_prompt, x_sample.reshape(1, nseq, D)
    outs = [[] for _ in range(8)]
    for l in range(depth):
        li = _lambda_init(l)
        w_in_b = w_in[l].astype(BF16)
        lam_p = jnp.stack([lam_q1[l], lam_k1[l], lam_q2[l], lam_k2[l]]).astype(F32)
        pad = jnp.zeros((D, LANES - N_GROUPS - N_EXPERTS), F32)
        lw = dict(
            wo=w_out[l].astype(BF16), ln1_g=ln1_g[l].reshape(1, D), ln1_b=ln1_b[l].reshape(1, D),
            wr=jnp.concatenate([w_group[l], w_expert[l], pad], axis=1).astype(BF16),
            br=jnp.concatenate([b_group[l], b_expert[l], pad[0]]).reshape(1, LANES),
            wg=w_gate[l].astype(BF16), wu=w_up[l].astype(BF16), wd=w_down[l].astype(BF16),
            ln2_g=ln2_g[l].reshape(1, D), ln2_b=ln2_b[l].reshape(1, D))
        gm, gd = g_moba[l].reshape(-1), g_diff[l].reshape(-1)

        qm_b, qd_b, km, vm, kd, vd, km_b, kd_b, vmt_b, vdt_b, kmean = _proj(xp, pos_p, w_in_b, prompt=True)
        om, od = _prompt_attention(lam_p, qm_b, km_b, vmt_b, kmean, gm, qd_b, kd_b, vdt_b, gd, li)
        x1, gates = _mix(xp, om, od, lw, alpha)
        ffn = (lw["wg"], lw["wu"], lw["wd"], lw["ln2_g"], lw["ln2_b"], alpha)
        for dst, a in zip(outs[:4], (km, vm, kd, vd)):
            dst.append(a)

        qm2, qd2, km2, vm2, kd2, vd2 = _proj(xs, pos_s, w_in_b, prompt=False)
        caches = [jnp.swapaxes(c[l].reshape(n_phys, page, w), 1, 2)
                  for c in (cache_moba_k, cache_moba_v, cache_diff_k)]
        caches.append(cache_diff_v[l].reshape(n_phys, page * DIFF_HEADS, DIFF_V_DIM))
        per_seq = [a.reshape(nseq, 1, w) for a in (qm2, qd2, km2, vm2, kd2, vd2)]
        ups = _decode_fits_moe(B * S, page_table, caches)
        if ups:
            y, om2, od2 = _moe_decode(x1, gates, *ffn, page_table, lam_p, *per_seq, caches, gm, gd, li, ups)
        else:
            y = _moe(x1, gates, *ffn)
            om2, od2 = _decode_attention(page_table, lam_p, *per_seq, caches, gm, gd, li)
        xp = y.reshape(B, S, D)
        x1s, gates_s = _mix(xs, om2.reshape(1, nseq, w), od2.reshape(1, nseq, w), lw, alpha)
        xs = _moe(x1s, gates_s, *ffn).reshape(1, nseq, D)
        for dst, a in zip(outs[4:], (km2, vm2, kd2, vd2)):
            dst.append(a)

    def stack(parts, tail, lead):
        return jnp.stack(parts).reshape((depth,) + lead + tail)

    mh, dh = (MOBA_HEADS, HEAD_DIM), (DIFF_HEADS, 2, DIFF_QK_DIM)
    dv = (DIFF_HEADS, DIFF_V_DIM)
    return (xp, xs.reshape(nseq, 1, D),
            stack(outs[0], mh, (B, S)), stack(outs[1], mh, (B, S)), stack(outs[2], dh, (B, S)), stack(outs[3], dv, (B, S)),
            stack(outs[4], mh, (nseq, 1)), stack(outs[5], mh, (nseq, 1)), stack(outs[6], dh, (nseq, 1)),
            stack(outs[7], dv, (nseq, 1)))
```

```python
import functools
import math

import jax
import jax.numpy as jnp
from jax import lax
from jax.experimental import pallas as pl
from jax.experimental.pallas import tpu as pltpu

F32 = jnp.float32
BF16 = jnp.bfloat16

LANES = 128
HEAD_DIM = 64
MOBA_HEADS = 8
MOBA_BLOCK = 256
MOBA_TOPK = 3
DIFF_HEADS = 4
DIFF_QK_DIM = 64
DIFF_V_DIM = 128
GROUP_WIDTH = 512
ROT_DIM = HEAD_DIM // 4
ROT_HALF = ROT_DIM // 2
ROPE_THETA = 500000.0
N_GROUPS = 4
EXPERTS_PER_GROUP = 4
N_EXPERTS = N_GROUPS * EXPERTS_PER_GROUP
EXPERT_TOPK = 2
LN_EPS = 1e-5
RMS_EPS = 1e-5
QK_SCALE = HEAD_DIM ** -0.5 * math.log2(math.e)
MASK_BIAS = 2.0 ** 30
ONES_ROWS = 16
KV_UNROLL = 2
ATTN_CHAINS = 4
VMEM_LIMIT = 48 * 1024 * 1024
VMEM_LIMIT_FUSED = 56 * 1024 * 1024
MOE_TILE = 1024
OUTPROJ_TILE = 1024
OUTPROJ_ROWS = 256


def _lambda_init(layer):
    return 0.8 - 0.6 * math.exp(-0.3 * layer)


def _params(sem, vmem=VMEM_LIMIT):
    return pltpu.CompilerParams(dimension_semantics=sem, vmem_limit_bytes=vmem)


def _nt_dot(a, b):
    return lax.dot_general(a, b, (((1,), (1,)), ((), ())), preferred_element_type=F32)


def _layer_norm(z, g, b):
    mu = jnp.mean(z, axis=-1, keepdims=True)
    zc = z - mu
    var = jnp.mean(zc * zc, axis=-1, keepdims=True)
    return zc * lax.rsqrt(var + LN_EPS) * g + b


def _rope_tables(pos):
    n = pos.shape[0]
    inv_freq = ROPE_THETA ** (-jnp.arange(ROT_HALF, dtype=F32) / ROT_HALF)
    ang = pos.astype(F32)[:, None] * inv_freq[None, :]
    cos, sin = jnp.cos(ang), jnp.sin(ang)
    rest = HEAD_DIM - ROT_DIM
    c64 = jnp.concatenate([cos, cos, jnp.ones((n, rest), F32)], axis=1)
    sa64 = jnp.concatenate([-sin, jnp.zeros((n, rest + ROT_HALF), F32)], axis=1)
    sb64 = jnp.concatenate([jnp.zeros((n, ROT_HALF), F32), sin, jnp.zeros((n, rest), F32)], axis=1)
    reps = LANES // HEAD_DIM
    return jnp.tile(c64, (1, reps)), jnp.tile(sa64, (1, reps)), jnp.tile(sb64, (1, reps))


def _rope(h, cos, sa, sb):
    parts = []
    for a in range(h.shape[1] // LANES):
        ha = h[:, a * LANES:(a + 1) * LANES]
        parts.append(ha * cos + pltpu.roll(ha, LANES - ROT_HALF, 1) * sa + pltpu.roll(ha, ROT_HALF, 1) * sb)
    return jnp.concatenate(parts, axis=1)


def _proj_kernel(x_ref, w_ref, cos_ref, sa_ref, sb_ref, *out_refs, prompt):
    if prompt:
        (qm_ref, qd_ref, km_ref, vm_ref, kd_ref, vd_ref,
         kmb_ref, kdb_ref, vmt_ref, vdt_ref, kmean_ref) = out_refs
    else:
        qm_ref, qd_ref, km_ref, vm_ref, kd_ref, vd_ref = out_refs
    xb = x_ref[0].astype(BF16)
    cos, sa, sb = cos_ref[...], sa_ref[...], sb_ref[...]
    w = GROUP_WIDTH

    def col(c):
        return jnp.dot(xb, w_ref[:, c * w:(c + 1) * w], preferred_element_type=F32)

    qm_ref[0] = (_rope(col(0), cos, sa, sb) * QK_SCALE).astype(BF16)
    km = _rope(col(1), cos, sa, sb)
    km_ref[0] = km
    vm = col(2)
    vm_ref[0] = vm
    qd_ref[0] = (_rope(col(3), cos, sa, sb) * QK_SCALE).astype(BF16)
    kd = _rope(col(4), cos, sa, sb)
    kd_ref[0] = kd
    vd = col(5)
    if prompt:
        for h in range(DIFF_HEADS):
            vd_ref[0, pl.ds(h, vd.shape[0], stride=DIFF_HEADS), :] = vd[:, h * DIFF_V_DIM:(h + 1) * DIFF_V_DIM]
        kmb_ref[0] = km.astype(BF16)
        kdb_ref[0] = kd.astype(BF16)
        kmean_ref[0, 0] = jnp.mean(km, axis=0, keepdims=True)
        vmt_ref[0, 0] = vm.T.astype(BF16)
        vdt_ref[0, 0] = vd.T.astype(BF16)
    else:
        vd_ref[0] = vd


def _proj(x, pos, w_in_b, *, prompt):
    B, S, D = x.shape
    tm = min(S, MOBA_BLOCK)
    nt = S // tm
    w = GROUP_WIDTH
    cos, sa, sb = _rope_tables(pos)
    row_spec = pl.BlockSpec((1, tm, w), lambda b, t: (b, t, 0))
    tab_spec = pl.BlockSpec((tm, LANES), lambda b, t: (t, 0))
    out_shape = [jax.ShapeDtypeStruct((B, S, w), BF16)] * 2 + [jax.ShapeDtypeStruct((B, S, w), F32)] * 4
    out_specs = [row_spec] * 6
    if prompt:
        out_shape[5] = jax.ShapeDtypeStruct((B, S * DIFF_HEADS, DIFF_V_DIM), F32)
        out_specs[5] = pl.BlockSpec((1, tm * DIFF_HEADS, DIFF_V_DIM), lambda b, t: (b, t, 0))
        vt_spec = pl.BlockSpec((1, 1, w, tm), lambda b, t: (b, t, 0, 0))
        out_shape += [jax.ShapeDtypeStruct((B, S, w), BF16)] * 2
        out_shape += [jax.ShapeDtypeStruct((B, nt, w, tm), BF16)] * 2
        out_shape += [jax.ShapeDtypeStruct((B, nt, 1, w), F32)]
        out_specs += [row_spec] * 2 + [vt_spec] * 2 + [pl.BlockSpec((1, 1, 1, w), lambda b, t: (b, t, 0, 0))]
    return pl.pallas_call(
        functools.partial(_proj_kernel, prompt=prompt),
        grid=(B, nt),
        in_specs=[pl.BlockSpec((1, tm, D), lambda b, t: (b, t, 0)),
                  pl.BlockSpec(w_in_b.shape, lambda b, t: (0, 0)),
                  tab_spec, tab_spec, tab_spec],
        out_specs=out_specs,
        out_shape=out_shape,
        compiler_params=_params(("parallel", "parallel")),
        name="proj_prompt" if prompt else "proj_sample",
    )(x, w_in_b, cos, sa, sb)


def _group_update(carry, get_score, vts, causal_last=False, next_score=None):
    nblk = len(vts[0])
    pairs = [(c, b) for c in range(len(vts)) for b in range(nblk)]
    maxes, parts = {}, {}
    for k in range(0, len(pairs), 2):
        duo = pairs[k:k + 2]
        if next_score is not None:
            for c, b in duo:
                next_score(c, b)
        probs = {}
        for c, b in duo:
            s = get_score(c, b)
            if causal_last and b == nblk - 1:
                krow = lax.broadcasted_iota(jnp.int32, s.shape, 0)
                qcol = lax.broadcasted_iota(jnp.int32, s.shape, 1)
                s = jnp.where(krow <= qcol, s, -jnp.inf)
            mb = jnp.max(s, axis=0, keepdims=True)
            probs[c, b] = jnp.exp2(s - mb).astype(BF16)
            maxes[c, b] = mb
        for c, b in duo:
            vt = vts[c][b]
            lhs = jnp.concatenate([vt, jnp.ones((ONES_ROWS, vt.shape[1]), BF16)], axis=0)
            parts[c, b] = jnp.dot(lhs, probs[c, b], preferred_element_type=F32)
    out = []
    for c, (m, acc) in enumerate(carry):
        m_new = functools.reduce(jnp.maximum, [maxes[c, b] for b in range(nblk)], m)
        acc = acc * jnp.exp2(m - m_new)
        for b in range(nblk):
            acc = acc + parts[c, b] * jnp.exp2(maxes[c, b] - m_new)
        out.append((m_new, acc))
    return tuple(out)


def _flash(i, nb, tq, vdims, qs, keys_of, values_of, buf):
    buf_a, buf_b = buf.at[0], buf.at[1]
    nchain = len(qs)
    init = tuple((jnp.full((1, tq), -jnp.inf, F32), jnp.zeros((e + ONES_ROWS, tq), F32)) for e in vdims)

    def blocks(g):
        return [jnp.minimum(g * KV_UNROLL + b, nb - 1) for b in range(KV_UNROLL)]

    def scorer(g, dst):
        ks = keys_of(blocks(g))

        def score_tile(c, b):
            dst[c * KV_UNROLL + b] = jnp.dot(ks[c][b], qs[c](), preferred_element_type=F32)
        return score_tile

    def update(carry, src, g, dst):
        return _group_update(carry, lambda c, b: src[c * KV_UNROLL + b], values_of(blocks(g)),
                             next_score=scorer(g + 1, dst))

    def last_update(carry, src, g, nvis):
        return _group_update(carry, lambda c, b: src[c * KV_UNROLL + b], values_of(blocks(g)[:nvis]),
                             causal_last=True)

    nfull = i // KV_UNROLL

    def two_groups(t, carry):
        carry = update(carry, buf_a, 2 * t, buf_b)
        return update(carry, buf_b, 2 * t + 1, buf_a)

    first = scorer(0, buf_a)
    for c in range(nchain):
        for b in range(KV_UNROLL):
            first(c, b)
    carry = lax.fori_loop(0, nfull // 2, two_groups, init)
    odd = nfull % 2
    carry = lax.cond(odd == 1, lambda carry: update(carry, buf_a, nfull - 1, buf_b), lambda carry: carry, carry)
    src = buf.at[odd]
    carry = lax.switch(i - nfull * KV_UNROLL,
                       [functools.partial(last_update, src=src, g=nfull, nvis=n + 1) for n in range(KV_UNROLL)], carry)
    return [acc[:e] / acc[e:e + 1] for (_, acc), e in zip(carry, vdims)]


def _topk_rows(g, k):
    row = lax.broadcasted_iota(jnp.int32, g.shape, 0).astype(F32)
    none = float(g.shape[0])
    sel = jnp.zeros(g.shape, F32)
    for _ in range(k):
        mx = jnp.max(g, axis=0, keepdims=True)
        idx = jnp.min(jnp.where(g == mx, row, none), axis=0, keepdims=True)
        idx = jnp.where(mx > -jnp.inf, idx, none)
        pick = row == idx
        sel = jnp.where(pick, 1.0, sel)
        g = jnp.where(pick, -jnp.inf, g)
    return sel


def _lambda_value(lam_ref, lam_init):
    lp = lam_ref[...]
    a = jnp.sum(lp[0:1] * lp[1:2], axis=1, keepdims=True)
    b = jnp.sum(lp[2:3] * lp[3:4], axis=1, keepdims=True)
    return jnp.exp(a) - jnp.exp(b) + lam_init


def _prompt_attn_kernel(lam_ref, qm_ref, km_ref, vmt_ref, kmean_ref, gm_ref, qd_ref, kd_ref, vdt_ref, gd_ref,
                        om_ref, od_ref, buf, *, lam_init):
    i = pl.program_id(2)
    nb = km_ref.shape[1]
    tq = qm_ref.shape[1]
    lane = lax.broadcasted_iota(jnp.int32, (1, LANES), 1)

    half = lax.broadcasted_iota(jnp.int32, (LANES, tq), 0) >> 6
    qmt = qm_ref[0].astype(F32).T
    kmean = kmean_ref[0].astype(BF16)
    brow = lax.broadcasted_iota(jnp.int32, (nb, tq), 0)
    qhts = [jnp.where(half == h, qmt, 0.0).astype(BF16) for h in range(LANES // HEAD_DIM)]
    gates = [jnp.dot(kmean, qht, preferred_element_type=F32) for qht in qhts]

    def moba_query(h):
        qht = qhts[h]
        sel = _topk_rows(jnp.where(brow < i, gates[h], -jnp.inf), MOBA_TOPK)
        bias_t = jnp.where(brow == i, 0.0, jnp.where(sel > 0.0, 0.0, -MASK_BIAS))
        if nb < LANES:
            bias_t = jnp.concatenate([bias_t, jnp.zeros((LANES - nb, tq), F32)], axis=0)
        return jnp.concatenate([qht, bias_t.astype(BF16)], axis=0)

    qdt = qd_ref[0].astype(F32).T
    qs = [functools.cache(functools.partial(lambda r: jnp.where(half == r, qdt, 0.0).astype(BF16), r))
          for r in range(2)]
    qs += [functools.cache(functools.partial(moba_query, h)) for h in range(LANES // HEAD_DIM)]

    def moba_keys(j):
        onehot = jnp.where(lane == j, 1.0, 0.0).astype(BF16)
        return jnp.concatenate([km_ref[0, j], jnp.broadcast_to(onehot, (MOBA_BLOCK, LANES))], axis=1)

    def keys_of(js):
        mk = [moba_keys(j) for j in js]
        dk = [kd_ref[0, j] for j in js]
        return [dk, dk, mk, mk]

    def values_of(js):
        dv = [vdt_ref[0, j] for j in js]
        return [dv, dv] + [[vmt_ref[0, j, h * HEAD_DIM:(h + 1) * HEAD_DIM, :] for j in js] for h in range(2)]

    d0, d1, o0, o1 = _flash(i, nb, tq, (DIFF_V_DIM, DIFF_V_DIM, HEAD_DIM, HEAD_DIM), qs, keys_of, values_of,
                            buf)

    outs = []
    for o in (o0, o1):
        ms = jnp.mean(o * o, axis=0, keepdims=True)
        outs.append(o * lax.rsqrt(ms + RMS_EPS))
    om_ref[0] = (jnp.concatenate(outs, axis=0).T * gm_ref[...]).astype(BF16)

    o = d0 - _lambda_value(lam_ref, lam_init) * d1
    ms = jnp.mean(o * o, axis=0, keepdims=True)
    o = o * lax.rsqrt(ms + RMS_EPS)
    od_ref[0] = ((o.T * gd_ref[...]) * (1.0 - lam_init)).astype(BF16)


def _prompt_attention(lam_p, qm_b, km_b, vmt_b, kmean, g_moba, qd_b, kd_b, vdt_b, g_diff, lam_init):
    B, S, w = qm_b.shape
    nb = S // MOBA_BLOCK
    assert nb % KV_UNROLL == 0 and w // LANES == DIFF_HEADS
    tq = MOBA_BLOCK
    q_spec = pl.BlockSpec((1, tq, LANES), lambda b, p, i: (b, i, p))
    k_spec = pl.BlockSpec((1, nb, tq, LANES), lambda b, p, i: (b, 0, 0, p))
    vt_spec = pl.BlockSpec((1, nb, LANES, tq), lambda b, p, i: (b, 0, p, 0))
    gain = pl.BlockSpec((1, LANES), lambda b, p, i: (0, p))
    return pl.pallas_call(
        functools.partial(_prompt_attn_kernel, lam_init=lam_init),
        grid=(B, DIFF_HEADS, nb),
        in_specs=[pl.BlockSpec(lam_p.shape, lambda b, p, i: (0, 0)),
                  q_spec, k_spec, vt_spec, pl.BlockSpec((1, nb, LANES), lambda b, p, i: (b, 0, p)), gain,
                  q_spec, k_spec, vt_spec, gain],
        out_specs=[q_spec, q_spec],
        out_shape=[jax.ShapeDtypeStruct((B, S, w), BF16)] * 2,
        scratch_shapes=[pltpu.VMEM((2, ATTN_CHAINS * KV_UNROLL, tq, tq), F32)],
        compiler_params=_params(("parallel", "parallel", "arbitrary")),
        name="prompt_attention",
    )(lam_p, qm_b, km_b.reshape(B, nb, tq, w), vmt_b, kmean.reshape(B, nb, w), g_moba.reshape(1, w),
      qd_b, kd_b.reshape(B, nb, tq, w), vdt_b, g_diff.reshape(1, w))


SROWS = 16
DECODE_BLOCKS_PER_STEP = 2


def _block_scores(q_ref, kt_refs):
    r = lax.broadcasted_iota(jnp.int32, (SROWS, GROUP_WIDTH), 0)
    c = lax.broadcasted_iota(jnp.int32, (SROWS, GROUP_WIDTH), 1)
    qrow = jnp.broadcast_to(q_ref[0].astype(F32), (SROWS, GROUP_WIDTH))
    qmat = jnp.where((c >> 6) == r, qrow, 0.0).astype(BF16)
    kt = jnp.concatenate([kr[0].astype(BF16) for kr in kt_refs], axis=1)
    return jnp.dot(qmat, kt, preferred_element_type=F32)


def _block_values(pb, v_refs, v_by_head):
    page = pb.shape[1] // len(v_refs)
    if v_by_head:
        v = jnp.concatenate(
            [jnp.concatenate([vr[0, pl.ds(h, page, stride=DIFF_HEADS), :].astype(BF16) for h in range(DIFF_HEADS)],
                             axis=1) for vr in v_refs], axis=0)
        return jnp.dot(pb, v, preferred_element_type=F32)
    vt = jnp.concatenate([vr[0].astype(BF16) for vr in v_refs], axis=1)
    return _nt_dot(pb, vt)


def _topk_list(gs, k):
    none = float(len(gs))
    gs = list(gs)
    sel = [jnp.zeros(g.shape, F32) for g in gs]
    for _ in range(k):
        mx = functools.reduce(jnp.maximum, gs)
        idx = functools.reduce(jnp.minimum, [jnp.where(g == mx, float(b), none) for b, g in enumerate(gs)])
        idx = jnp.where(mx > -jnp.inf, idx, none)
        for b in range(len(gs)):
            pick = idx == float(b)
            sel[b] = jnp.where(pick, 1.0, sel[b])
            gs[b] = jnp.where(pick, -jnp.inf, gs[b])
    return sel


def _merge_partials(sel, m_sc, l_sc, r_sc, s_new, v_new, nblk):
    m = s_new
    for j in range(nblk):
        mj = m_sc[j][:, 0:1]
        m = jnp.maximum(m, mj if sel is None else jnp.where(sel[j] > 0.0, mj, -jnp.inf))
    w_new = jnp.exp2(s_new - m)
    l = w_new
    racc = w_new * v_new
    for j in range(nblk):
        wj = jnp.exp2(m_sc[j][:, 0:1] - m)
        if sel is not None:
            wj = jnp.where(sel[j] > 0.0, wj, 0.0)
        l = l + wj * l_sc[j][:, 0:1]
        racc = racc + wj * r_sc[j]
    return racc / l


def _decode_scores(qm_ref, qd_ref, pages, *, ppb, bps):
    npg = ppb * bps
    mk, _, dk, _ = (pages[a * npg:(a + 1) * npg] for a in range(4))
    return {(u, diff): _block_scores(qd_ref if diff else qm_ref, (dk if diff else mk)[u * ppb:(u + 1) * ppb])
            for u in range(bps) for diff in (False, True)}


def _decode_finish(step, scores, pages, stats_sc, *, ppb, bps):
    npg = ppb * bps
    gm_sc, mm_sc, lm_sc, rm_sc, md_sc, ld_sc, rd_sc = stats_sc
    _, mv, _, dv = (pages[a * npg:(a + 1) * npg] for a in range(4))
    stats = {}
    for key, s in scores.items():
        m = jnp.max(s, axis=1, keepdims=True)
        p = jnp.exp2(s - m)
        stats[key] = (jnp.sum(s, axis=1, keepdims=True), m, jnp.sum(p, axis=1, keepdims=True), p.astype(BF16))
    for (u, diff), (g, m, l, pb) in stats.items():
        r = _block_values(pb, (dv if diff else mv)[u * ppb:(u + 1) * ppb], diff)
        j = step * bps + u
        m_sc, l_sc, r_sc = (md_sc, ld_sc, rd_sc) if diff else (mm_sc, lm_sc, rm_sc)
        m_sc[j] = jnp.broadcast_to(m, (SROWS, LANES))
        l_sc[j] = jnp.broadcast_to(l, (SROWS, LANES))
        r_sc[j] = r
        if not diff:
            gm_sc[j] = jnp.broadcast_to(g, (SROWS, LANES))


def _decode_merge(lam_ref, qm_ref, qd_ref, kmn_ref, vmn_ref, kdn_ref, vdn_ref, gm_ref, gd_ref, om_ref, od_ref,
                  stats_sc, *, lam_init, nblk):
    gm_sc, mm_sc, lm_sc, rm_sc, md_sc, ld_sc, rd_sc = stats_sc
    r16 = lax.broadcasted_iota(jnp.int32, (SROWS, GROUP_WIDTH), 0)
    c16 = lax.broadcasted_iota(jnp.int32, (SROWS, GROUP_WIDTH), 1)

    def new_score(q_ref, k_ref):
        qf = jnp.where((c16 >> 6) == r16, jnp.broadcast_to(q_ref[0].astype(F32), (SROWS, GROUP_WIDTH)), 0.0)
        return jnp.sum(qf * k_ref[0], axis=1, keepdims=True)

    sel = _topk_list([gm_sc[b][:, 0:1] for b in range(nblk)], MOBA_TOPK)
    o = _merge_partials(sel, mm_sc, lm_sc, rm_sc, new_score(qm_ref, kmn_ref), vmn_ref[0], nblk)
    own = (c16 >> 6) == r16
    ms = jnp.sum(jnp.where(own, o * o, 0.0), axis=1, keepdims=True) * (1.0 / HEAD_DIM)
    o = jnp.where(own, o * lax.rsqrt(ms + RMS_EPS), 0.0)
    om_ref[0] = (jnp.sum(o, axis=0, keepdims=True) * gm_ref[...]).astype(BF16)

    lam = _lambda_value(lam_ref, lam_init)
    o = _merge_partials(None, md_sc, ld_sc, rd_sc, new_score(qd_ref, kdn_ref), vdn_ref[0], nblk)
    signed = jnp.where((r16 & 1) == 0, o, -lam * o)
    o = jnp.sum(jnp.where((c16 >> 7) == (r16 >> 1), signed, 0.0), axis=0, keepdims=True)
    segs = []
    for h in range(DIFF_HEADS):
        seg = o[:, h * DIFF_V_DIM:(h + 1) * DIFF_V_DIM]
        ms = jnp.mean(seg * seg, axis=1, keepdims=True)
        segs.append(seg * lax.rsqrt(ms + RMS_EPS))
    od_ref[0] = ((jnp.concatenate(segs, axis=1) * gd_ref[...]) * (1.0 - lam_init)).astype(BF16)


def _decode_kernel(pt_ref, *refs, lam_init, ppb, bps, nblk):
    del pt_ref
    npg = ppb * bps
    seq_refs, pages = refs[:9], refs[9:9 + 4 * npg]
    om_ref, od_ref = refs[9 + 4 * npg:11 + 4 * npg]
    stats_sc = refs[11 + 4 * npg:]
    step = pl.program_id(1)
    scores = _decode_scores(seq_refs[1], seq_refs[2], pages, ppb=ppb, bps=bps)
    _decode_finish(step, scores, pages, stats_sc, ppb=ppb, bps=bps)

    @pl.when(step == nblk // bps - 1)
    def _():
        _decode_merge(*seq_refs, om_ref, od_ref, stats_sc, lam_init=lam_init, nblk=nblk)


def _decode_geometry(page_table, caches):
    nseq, npages = page_table.shape
    page = caches[0].shape[2]
    ppb = MOBA_BLOCK // page
    nblk = npages // ppb
    bps = DECODE_BLOCKS_PER_STEP if nblk % DECODE_BLOCKS_PER_STEP == 0 else 1
    return nseq, npages, ppb, nblk, bps


def _decode_scratch(nblk):
    stat = pltpu.VMEM((nblk, SROWS, LANES), F32)
    part = pltpu.VMEM((nblk, SROWS, GROUP_WIDTH), F32)
    return [stat, stat, stat, part, stat, stat, part]


def _decode_attention(page_table, lam_p, qm_b, qd_b, km, vm, kd, vd, caches, g_moba, g_diff, lam_init):
    nseq, npages, ppb, nblk, bps = _decode_geometry(page_table, caches)
    npg = ppb * bps
    w = GROUP_WIDTH
    row = pl.BlockSpec((1, 1, w), lambda b, j, pt: (b, 0, 0))
    gain = pl.BlockSpec((1, w), lambda b, j, pt: (0, 0))

    def page_spec(shape, a):
        return pl.BlockSpec((1,) + shape[1:], lambda b, j, pt, a=a: (pt[b * npages + j * npg + a], 0, 0))

    page_specs, page_args = [], []
    for cache in caches:
        for a in range(npg):
            page_specs.append(page_spec(cache.shape, a))
            page_args.append(cache)
    return pl.pallas_call(
        functools.partial(_decode_kernel, lam_init=lam_init, ppb=ppb, bps=bps, nblk=nblk),
        grid_spec=pltpu.PrefetchScalarGridSpec(
            num_scalar_prefetch=1,
            grid=(nseq, nblk // bps),
            in_specs=[pl.BlockSpec(lam_p.shape, lambda b, j, pt: (0, 0)), row, row, row, row, row, row, gain, gain]
            + page_specs,
            out_specs=[row, row],
            scratch_shapes=_decode_scratch(nblk),
        ),
        out_shape=[jax.ShapeDtypeStruct((nseq, 1, w), BF16)] * 2,
        compiler_params=_params(("parallel", "arbitrary")),
        name="decode_attention",
    )(page_table.reshape(-1), lam_p, qm_b, qd_b, km, vm, kd, vd, g_moba.reshape(1, w), g_diff.reshape(1, w),
      *page_args)


def _route(logits):
    lane = lax.broadcasted_iota(jnp.int32, (1, LANES), 1).astype(F32)
    none = float(LANES)
    gl = jnp.where(lane < N_GROUPS, logits, -jnp.inf)
    gmax = jnp.max(gl, axis=1, keepdims=True)
    gidx = jnp.min(jnp.where(gl == gmax, lane, none), axis=1, keepdims=True)
    g_w = 1.0 / jnp.sum(jnp.exp(gl - gmax), axis=1, keepdims=True)
    lo = N_GROUPS + EXPERTS_PER_GROUP * gidx
    el = jnp.where(lane >= lo, jnp.where(lane < lo + EXPERTS_PER_GROUP, logits, -jnp.inf), -jnp.inf)
    e1 = jnp.max(el, axis=1, keepdims=True)
    i1 = jnp.min(jnp.where(el == e1, lane, none), axis=1, keepdims=True)
    el2 = jnp.where(lane == i1, -jnp.inf, el)
    e2 = jnp.max(el2, axis=1, keepdims=True)
    i2 = jnp.min(jnp.where(el2 == e2, lane, none), axis=1, keepdims=True)
    t2 = jnp.exp(e2 - e1)
    w1 = g_w / (1.0 + t2)
    return jnp.where(lane == i1, w1, 0.0) + jnp.where(lane == i2, w1 * t2, 0.0)


def _outproj_kernel(x_ref, om_ref, od_ref, wo_ref, g_ref, b_ref, wr_ref, br_ref, x1_ref, gates_ref, *, alpha):
    w = GROUP_WIDTH
    nsub = max(1, x_ref.shape[0] // OUTPROJ_ROWS)
    subs = [pl.ds(s * (x_ref.shape[0] // nsub), x_ref.shape[0] // nsub) for s in range(nsub)]
    ys = [jnp.dot(om_ref[r, :], wo_ref[0:w, :], preferred_element_type=F32)
          + jnp.dot(od_ref[r, :], wo_ref[w:2 * w, :], preferred_element_type=F32) for r in subs]
    for r, y in zip(subs, ys):
        x1 = _layer_norm(alpha * x_ref[r, :] + y, g_ref[...], b_ref[...])
        x1_ref[r, :] = x1
        gates_ref[r, :] = _route(jnp.dot(x1.astype(BF16), wr_ref[...], preferred_element_type=F32) + br_ref[...])


def _outproj(x2, om, od, wo_b, ln_g, ln_b, wr_b, br, alpha):
    N, D = x2.shape
    tm = min(N, OUTPROJ_TILE)
    w = GROUP_WIDTH
    full = lambda a: pl.BlockSpec(a.shape, lambda t: (0, 0))
    return pl.pallas_call(
        functools.partial(_outproj_kernel, alpha=alpha),
        grid=(N // tm,),
        in_specs=[pl.BlockSpec((tm, D), lambda t: (t, 0)),
                  pl.BlockSpec((tm, w), lambda t: (t, 0)),
                  pl.BlockSpec((tm, w), lambda t: (t, 0)),
                  full(wo_b), full(ln_g), full(ln_b), full(wr_b), full(br)],
        out_specs=[pl.BlockSpec((tm, D), lambda t: (t, 0)), pl.BlockSpec((tm, LANES), lambda t: (t, 0))],
        out_shape=[jax.ShapeDtypeStruct((N, D), F32), jax.ShapeDtypeStruct((N, LANES), F32)],
        compiler_params=_params(("parallel",)),
        name="outproj_router",
    )(x2, om, od, wo_b, ln_g, ln_b, wr_b, br)


def _moe_kernel(x1_ref, gates_ref, wg_ref, wu_ref, wd_ref, g_ref, b_ref, o_ref, xb_sc, acc_sc, *, alpha):
    e = pl.program_id(1)

    @pl.when(e == 0)
    def _():
        xb_sc[...] = x1_ref[...].astype(BF16)
        acc_sc[...] = jnp.zeros_like(acc_sc)

    xb = xb_sc[...]
    a = jnp.dot(xb, wg_ref[0], preferred_element_type=F32)
    u = jnp.dot(xb, wu_ref[0], preferred_element_type=F32)
    lane = lax.broadcasted_iota(jnp.int32, (1, LANES), 1)
    gate = jnp.sum(jnp.where(lane == e + N_GROUPS, gates_ref[...], 0.0), axis=1, keepdims=True)
    hid = (a / (1.0 + jnp.exp(-a))) * u * gate
    acc_sc[...] += jnp.dot(hid.astype(BF16), wd_ref[0], preferred_element_type=F32)

    @pl.when(e == N_EXPERTS - 1)
    def _():
        o_ref[...] = _layer_norm(alpha * x1_ref[...] + acc_sc[...], g_ref[...], b_ref[...])


def _moe(x1, gates, wg_b, wu_b, wd_b, ln_g, ln_b, alpha):
    N, D = x1.shape
    tm = min(N, MOE_TILE)
    f = wg_b.shape[2]
    full = lambda a: pl.BlockSpec(a.shape, lambda t, e: (0, 0))
    return pl.pallas_call(
        functools.partial(_moe_kernel, alpha=alpha),
        grid=(N // tm, N_EXPERTS),
        in_specs=[pl.BlockSpec((tm, D), lambda t, e: (t, 0)),
                  pl.BlockSpec((tm, LANES), lambda t, e: (t, 0)),
                  pl.BlockSpec((1, D, f), lambda t, e: (e, 0, 0)),
                  pl.BlockSpec((1, D, f), lambda t, e: (e, 0, 0)),
                  pl.BlockSpec((1, f, D), lambda t, e: (e, 0, 0)),
                  full(ln_g), full(ln_b)],
        out_specs=pl.BlockSpec((tm, D), lambda t, e: (t, 0)),
        out_shape=jax.ShapeDtypeStruct((N, D), F32),
        scratch_shapes=[pltpu.VMEM((tm, D), BF16), pltpu.VMEM((tm, D), F32)],
        compiler_params=_params(("parallel", "arbitrary")),
        name="moe_ffn",
    )(x1, gates, wg_b, wu_b, wd_b, ln_g, ln_b)


def _moe_decode_kernel(pt_ref, x1_ref, gates_ref, wg_ref, wu_ref, wd_ref, g_ref, b_ref, *refs,
                       alpha, lam_init, ppb, bps, nblk, ups):
    del pt_ref
    npg = ppb * bps
    seq_refs, pages = refs[:9], refs[9:9 + ups * 4 * npg]
    o_ref, om_ref, od_ref, xb_sc, acc_sc = refs[9 + ups * 4 * npg:14 + ups * 4 * npg]
    stats_sc = refs[14 + ups * 4 * npg:]
    e = pl.program_id(1)
    unit0 = (pl.program_id(0) * N_EXPERTS + e) * ups
    steps_per_seq = nblk // bps

    @pl.when(e == 0)
    def _():
        xb_sc[...] = x1_ref[...].astype(BF16)
        acc_sc[...] = jnp.zeros_like(acc_sc)

    joined = [pages[(k * 4 + cache) * npg + a] for cache in range(4) for k in range(ups) for a in range(npg)]
    scores = _decode_scores(seq_refs[1], seq_refs[2], joined, ppb=ppb, bps=bps * ups)
    xb = xb_sc[...]
    a = jnp.dot(xb, wg_ref[0], preferred_element_type=F32)
    u = jnp.dot(xb, wu_ref[0], preferred_element_type=F32)
    _decode_finish((unit0 % steps_per_seq) // ups, scores, joined, stats_sc, ppb=ppb, bps=bps * ups)
    lane = lax.broadcasted_iota(jnp.int32, (1, LANES), 1)
    gate = jnp.sum(jnp.where(lane == e + N_GROUPS, gates_ref[...], 0.0), axis=1, keepdims=True)
    hid = (a / (1.0 + jnp.exp(-a))) * u * gate
    acc_sc[...] += jnp.dot(hid.astype(BF16), wd_ref[0], preferred_element_type=F32)

    @pl.when(e == N_EXPERTS - 1)
    def _():
        o_ref[...] = _layer_norm(alpha * x1_ref[...] + acc_sc[...], g_ref[...], b_ref[...])

    @pl.when((unit0 + ups - 1) % steps_per_seq == steps_per_seq - 1)
    def _():
        _decode_merge(*seq_refs, om_ref, od_ref, stats_sc, lam_init=lam_init, nblk=nblk)


def _decode_fits_moe(n_tokens, page_table, caches):
    nseq, _, _, nblk, bps = _decode_geometry(page_table, caches)
    moe_steps = (n_tokens // min(n_tokens, MOE_TILE)) * N_EXPERTS
    total, per_seq = nseq * (nblk // bps), nblk // bps
    ups = total // moe_steps
    ok = ups >= 1 and ups * moe_steps == total and per_seq % ups == 0 and ups <= 2
    return ups if ok else 0


def _moe_decode(x1, gates, wg_b, wu_b, wd_b, ln_g, ln_b, alpha,
                page_table, lam_p, qm_b, qd_b, km, vm, kd, vd, caches, g_moba, g_diff, lam_init, ups):
    N, D = x1.shape
    tm = min(N, MOE_TILE)
    f = wg_b.shape[2]
    nseq, npages, ppb, nblk, bps = _decode_geometry(page_table, caches)
    npg = ppb * bps
    per_seq = nblk // bps
    w = GROUP_WIDTH
    full = lambda a: pl.BlockSpec(a.shape, lambda t, e, pt: (0, 0))
    seq_of = lambda t, e: ((t * N_EXPERTS + e) * ups) // per_seq
    row = pl.BlockSpec((1, 1, w), lambda t, e, pt: (seq_of(t, e), 0, 0))
    gain = pl.BlockSpec((1, w), lambda t, e, pt: (0, 0))

    def page_spec(shape, k, a):
        def index(t, e, pt):
            return (pt[((t * N_EXPERTS + e) * ups + k) * npg + a], 0, 0)
        return pl.BlockSpec((1,) + shape[1:], index)

    page_specs, page_args = [], []
    for k in range(ups):
        for cache in caches:
            for a in range(npg):
                page_specs.append(page_spec(cache.shape, k, a))
                page_args.append(cache)
    return pl.pallas_call(
        functools.partial(_moe_decode_kernel, alpha=alpha, lam_init=lam_init, ppb=ppb, bps=bps, nblk=nblk, ups=ups),
        grid_spec=pltpu.PrefetchScalarGridSpec(
            num_scalar_prefetch=1,
            grid=(N // tm, N_EXPERTS),
            in_specs=[pl.BlockSpec((tm, D), lambda t, e, pt: (t, 0)),
                      pl.BlockSpec((tm, LANES), lambda t, e, pt: (t, 0)),
                      pl.BlockSpec((1, D, f), lambda t, e, pt: (e, 0, 0)),
                      pl.BlockSpec((1, D, f), lambda t, e, pt: (e, 0, 0)),
                      pl.BlockSpec((1, f, D), lambda t, e, pt: (e, 0, 0)),
                      full(ln_g), full(ln_b),
                      pl.BlockSpec(lam_p.shape, lambda t, e, pt: (0, 0)), row, row, row, row, row, row, gain, gain]
            + page_specs,
            out_specs=[pl.BlockSpec((tm, D), lambda t, e, pt: (t, 0)), row, row],
            scratch_shapes=[pltpu.VMEM((tm, D), BF16), pltpu.VMEM((tm, D), F32)] + _decode_scratch(nblk),
        ),
        out_shape=[jax.ShapeDtypeStruct((N, D), F32)] + [jax.ShapeDtypeStruct((nseq, 1, w), BF16)] * 2,
        compiler_params=_params(("arbitrary", "arbitrary"), VMEM_LIMIT_FUSED),
        name="moe_ffn_decode",
    )(page_table.reshape(-1), x1, gates, wg_b, wu_b, wd_b, ln_g, ln_b,
      lam_p, qm_b, qd_b, km, vm, kd, vd, g_moba.reshape(1, w), g_diff.reshape(1, w), *page_args)


def _mix(x, om, od, lw, alpha):
    B, S, D = x.shape
    w = GROUP_WIDTH
    return _outproj(x.reshape(B * S, D), om.reshape(B * S, w), od.reshape(B * S, w),
                    lw["wo"], lw["ln1_g"], lw["ln1_b"], lw["wr"], lw["br"], alpha)


def kernel(x_prompt, x_sample, cache_moba_k, cache_moba_v, cache_diff_k, cache_diff_v, page_table, w_in, w_out, g_moba, g_diff, lam_q1, lam_k1, lam_q2, lam_k2, ln1_g, ln1_b, w_group, b_group, w_expert, b_expert, w_gate, w_up, w_down, ln2_g, ln2_b):
    depth = w_in.shape[0]
    B, S, D = x_prompt.shape
    nseq, dec_seq, _ = x_sample.shape
    assert dec_seq == 1 and S % MOBA_BLOCK == 0
    n_phys, page = cache_moba_k.shape[1:3]
    past_len = page_table.shape[1] * page
    assert past_len % MOBA_BLOCK == 0 and MOBA_BLOCK % page == 0
    w = GROUP_WIDTH
    alpha = (2 * depth) ** 0.25
    pos_p = jnp.arange(S)
    pos_s = jnp.full((nseq,), past_len, jnp.int32)

    xp, xs = x_prompt, x_sample.reshape(1, nseq, D)
    outs = [[] for _ in range(8)]
    for l in range(depth):
        li = _lambda_init(l)
        w_in_b = w_in[l].astype(BF16)
        lam_p = jnp.stack([lam_q1[l], lam_k1[l], lam_q2[l], lam_k2[l]]).astype(F32)
        pad = jnp.zeros((D, LANES - N_GROUPS - N_EXPERTS), F32)
        lw = dict(
            wo=w_out[l].astype(BF16), ln1_g=ln1_g[l].reshape(1, D), ln1_b=ln1_b[l].reshape(1, D),
            wr=jnp.concatenate([w_group[l], w_expert[l], pad], axis=1).astype(BF16),
            br=jnp.concatenate([b_group[l], b_expert[l], pad[0]]).reshape(1, LANES),
            wg=w_gate[l].astype(BF16), wu=w_up[l].astype(BF16), wd=w_down[l].astype(BF16),
            ln2_g=ln2_g[l].reshape(1, D), ln2_b=ln2_b[l].reshape(1, D))
        gm, gd = g_moba[l].reshape(-1), g_diff[l].reshape(-1)

        qm_b, qd_b, km, vm, kd, vd, km_b, kd_b, vmt_b, vdt_b, kmean = _proj(xp, pos_p, w_in_b, prompt=True)
        om, od = _prompt_attention(lam_p, qm_b, km_b, vmt_b, kmean, gm, qd_b, kd_b, vdt_b, gd, li)
        x1, gates = _mix(xp, om, od, lw, alpha)
        ffn = (lw["wg"], lw["wu"], lw["wd"], lw["ln2_g"], lw["ln2_b"], alpha)
        for dst, a in zip(outs[:4], (km, vm, kd, vd)):
            dst.append(a)

        qm2, qd2, km2, vm2, kd2, vd2 = _proj(xs, pos_s, w_in_b, prompt=False)
        caches = [jnp.swapaxes(c[l].reshape(n_phys, page, w), 1, 2)
                  for c in (cache_moba_k, cache_moba_v, cache_diff_k)]
        caches.append(cache_diff_v[l].reshape(n_phys, page * DIFF_HEADS, DIFF_V_DIM))
        per_seq = [a.reshape(nseq, 1, w) for a in (qm2, qd2, km2, vm2, kd2, vd2)]
        ups = _decode_fits_moe(B * S, page_table, caches)
        if ups:
            y, om2, od2 = _moe_decode(x1, gates, *ffn, page_table, lam_p, *per_seq, caches, gm, gd, li, ups)
        else:
            y = _moe(x1, gates, *ffn)
            om2, od2 = _decode_attention(page_table, lam_p, *per_seq, caches, gm, gd, li)
        xp = y.reshape(B, S, D)
        x1s, gates_s = _mix(xs, om2.reshape(1, nseq, w), od2.reshape(1, nseq, w), lw, alpha)
        xs = _moe(x1s, gates_s, *ffn).reshape(1, nseq, D)
        for dst, a in zip(outs[4:], (km2, vm2, kd2, vd2)):
            dst.append(a)

    def stack(parts, tail, lead):
        return jnp.stack(parts).reshape((depth,) + lead + tail)

    mh, dh = (MOBA_HEADS, HEAD_DIM), (DIFF_HEADS, 2, DIFF_QK_DIM)
    dv = (DIFF_HEADS, DIFF_V_DIM)
    return (xp, xs.reshape(nseq, 1, D),
            stack(outs[0], mh, (B, S)), stack(outs[1], mh, (B, S)), stack(outs[2], dh, (B, S)), stack(outs[3], dv, (B, S)),
            stack(outs[4], mh, (nseq, 1)), stack(outs[5], mh, (nseq, 1)), stack(outs[6], dh, (nseq, 1)),
            stack(outs[7], dv, (nseq, 1)))
```
